```python
import jax, jax.numpy as jnp
from jax import lax
import numpy as np

D_MODEL = 4096
BATCH = 2
SEQ = 4096
DEPTH = 4

N_MIXERS = 2
MLSTM_HEADS = 8
MLSTM_QK_HEAD_DIM = D_MODEL // 2 // MLSTM_HEADS
MLSTM_V_HEAD_DIM = D_MODEL // MLSTM_HEADS
MLSTM_QK_WIDTH = MLSTM_HEADS * MLSTM_QK_HEAD_DIM
MLSTM_V_WIDTH = MLSTM_HEADS * MLSTM_V_HEAD_DIM
MLSTM_IN_DIM = 2 * MLSTM_QK_WIDTH + 2 * MLSTM_V_WIDTH + 2 * MLSTM_HEADS
MLSTM_CHUNK = 64
GATE_SOFT_CAP = 15.0
HEAD_NORM_EPS = 1e-6
FOX_HEAD_DIM = 128
FOX_HEADS = D_MODEL // FOX_HEAD_DIM
FOX_WIDTH = FOX_HEADS * FOX_HEAD_DIM
FOX_IN_DIM = 3 * FOX_WIDTH + FOX_HEADS
FOX_Q_BLOCK = 128
FFN_DIM = ((8 * D_MODEL + 3 * 256 - 1) // (3 * 256)) * 256
ALPHA = (2 * DEPTH) ** 0.25
BETA = (8 * DEPTH) ** -0.25
LN_EPS = 1e-5
N_MLSTM_LAYERS = (DEPTH + N_MIXERS - 1) // N_MIXERS
N_FOX_LAYERS = DEPTH // N_MIXERS

kernel_name = 'hybrid_mlstm_fox_deepnorm'


def layer_norm(x, g, b):
    xf = x.astype(jnp.float32)
    mu = jnp.mean(xf, axis=-1, keepdims=True)
    xc = xf - mu
    var = jnp.mean(xc * xc, axis=-1, keepdims=True)
    y = xc * lax.rsqrt(var + LN_EPS) * g.astype(jnp.float32) + b.astype(jnp.float32)
    return y.astype(x.dtype)


def soft_cap(z):
    return GATE_SOFT_CAP * jnp.tanh(z / GATE_SOFT_CAP)


def mlstm_chunk_step(carry, inp):
    c_state, n_state, m_state = carry
    q, k, v, lf, ig = inp
    L = q.shape[2]
    causal = jnp.tril(jnp.ones((L, L), dtype=bool))
    b = jnp.cumsum(lf, axis=-1)
    d_log = b[..., :, None] - b[..., None, :] + ig[..., None, :]
    d_log = jnp.where(causal, d_log, -jnp.inf)
    g_log = b + m_state[..., None]
    m_t = jnp.maximum(g_log, jnp.max(d_log, axis=-1))
    w_intra = jnp.exp(d_log - m_t[..., None])
    w_inter = jnp.exp(g_log - m_t)
    s = jnp.einsum('bhtd,bhsd->bhts', q, k) * w_intra
    num = (jnp.einsum('bhts,bhsv->bhtv', s, v)
           + w_inter[..., None] * jnp.einsum('bhvd,bhtd->bhtv', c_state, q))
    den = jnp.sum(s, axis=-1) + w_inter * jnp.einsum('bhd,bhtd->bht', n_state, q)
    h = num / jnp.maximum(jnp.abs(den), jnp.exp(-m_t))[..., None]
    b_end = b[..., -1]
    a_log = b_end[..., None] - b + ig
    m_new = jnp.maximum(b_end + m_state, jnp.max(a_log, axis=-1))
    decay = jnp.exp(b_end + m_state - m_new)
    wa = jnp.exp(a_log - m_new[..., None])
    c_new = decay[..., None, None] * c_state + jnp.einsum('bhsv,bhsd->bhvd', wa[..., None] * v, k)
    n_new = decay[..., None] * n_state + jnp.einsum('bhs,bhsd->bhd', wa, k)
    return (c_new, n_new, m_new), h


def mlstm_mixer(x, w_in, b_gate, norm_g, w_out):
    B, S, _ = x.shape
    H, dk, dv, L = MLSTM_HEADS, MLSTM_QK_HEAD_DIM, MLSTM_V_HEAD_DIM, MLSTM_CHUNK
    nc = S // L
    proj = x @ w_in
    q, k, v, o, gates = jnp.split(
        proj, [MLSTM_QK_WIDTH, 2 * MLSTM_QK_WIDTH, 2 * MLSTM_QK_WIDTH + MLSTM_V_WIDTH,
               2 * MLSTM_QK_WIDTH + 2 * MLSTM_V_WIDTH], axis=-1)
    gates = soft_cap(gates.astype(jnp.float32) + b_gate.astype(jnp.float32))
    ig = gates[..., :H]
    lf = jax.nn.log_sigmoid(gates[..., H:])

    def to_chunks(t, d):
        return t.astype(jnp.float32).reshape(B, nc, L, H, d).transpose(1, 0, 3, 2, 4)

    def gate_chunks(t):
        return t.reshape(B, nc, L, H).transpose(1, 0, 3, 2)

    qc = to_chunks(q, dk) * (dk ** -0.5)
    kc = to_chunks(k, dk)
    vc = to_chunks(v, dv)
    init = (jnp.zeros((B, H, dv, dk), jnp.float32),
            jnp.zeros((B, H, dk), jnp.float32),
            jnp.zeros((B, H), jnp.float32))
    _, hc = lax.scan(mlstm_chunk_step, init, (qc, kc, vc, gate_chunks(lf), gate_chunks(ig)))
    h = hc.transpose(1, 0, 3, 2, 4).reshape(B, S, H, dv)
    h = h * lax.rsqrt(jnp.mean(h * h, axis=-1, keepdims=True) + HEAD_NORM_EPS)
    h = h.reshape(B, S, MLSTM_V_WIDTH) * norm_g.astype(jnp.float32)
    h = (h * jax.nn.sigmoid(o.astype(jnp.float32))).astype(x.dtype)
    return h @ w_out


def fox_mixer(x, w_in, b_f, w_out):
    B, S, _ = x.shape
    H, dh, Q = FOX_HEADS, FOX_HEAD_DIM, FOX_Q_BLOCK
    proj = x @ w_in
    q, k, v, fg = jnp.split(proj, [FOX_WIDTH, 2 * FOX_WIDTH, 3 * FOX_WIDTH], axis=-1)

    def heads(t):
        return t.reshape(B, S, H, dh).transpose(0, 2, 1, 3)

    q, k, v = heads(q), heads(k), heads(v)
    logf = jax.nn.log_sigmoid(fg.astype(jnp.float32) + b_f.astype(jnp.float32))
    cum = jnp.cumsum(logf.transpose(0, 2, 1), axis=-1)
    scale = dh ** -0.5
    outs = []
    for blk in range(S // Q):
        t0, t1 = blk * Q, (blk + 1) * Q
        logits = (jnp.einsum('bhtd,bhsd->bhts', q[:, :, t0:t1], k[:, :, :t1]).astype(jnp.float32) * scale
                  + cum[:, :, t0:t1, None] - cum[:, :, None, :t1])
        mask = (t0 + jnp.arange(Q))[:, None] >= jnp.arange(t1)[None, :]
        p = jax.nn.softmax(jnp.where(mask, logits, -jnp.inf), axis=-1)
        outs.append(jnp.einsum('bhts,bhsd->bhtd', p.astype(v.dtype), v[:, :, :t1]))
    o = jnp.concatenate(outs, axis=2).transpose(0, 2, 1, 3).reshape(B, S, FOX_WIDTH)
    return o @ w_out


def swiglu(x, w_up, w_down):
    gate, up = jnp.split(x @ w_up, 2, axis=-1)
    return (jax.nn.silu(gate) * up) @ w_down


def setup_inputs(seed: int = 0) -> dict:
    key = jax.random.key(seed)
    ks = jax.random.split(key, 16)
    nrm = jax.random.normal
    s_in = D_MODEL ** -0.5
    x = nrm(ks[0], (BATCH, SEQ, D_MODEL), jnp.float32)
    mlstm_w_in = nrm(ks[1], (N_MLSTM_LAYERS, D_MODEL, MLSTM_IN_DIM), jnp.float32) * s_in
    mlstm_b_gate = jnp.concatenate(
        [0.1 * nrm(ks[2], (N_MLSTM_LAYERS, MLSTM_HEADS), jnp.float32),
         3.0 + 0.5 * nrm(ks[3], (N_MLSTM_LAYERS, MLSTM_HEADS), jnp.float32)], axis=-1)
    mlstm_norm_g = 1.0 + 0.02 * nrm(ks[4], (N_MLSTM_LAYERS, MLSTM_V_WIDTH), jnp.float32)
    mlstm_w_out = nrm(ks[5], (N_MLSTM_LAYERS, MLSTM_V_WIDTH, D_MODEL), jnp.float32) * (MLSTM_V_WIDTH ** -0.5 * BETA)
    fox_w_in = nrm(ks[6], (N_FOX_LAYERS, D_MODEL, FOX_IN_DIM), jnp.float32) * s_in
    fox_b_f = 3.0 + 0.5 * nrm(ks[7], (N_FOX_LAYERS, FOX_HEADS), jnp.float32)
    fox_w_out = nrm(ks[8], (N_FOX_LAYERS, FOX_WIDTH, D_MODEL), jnp.float32) * (FOX_WIDTH ** -0.5 * BETA)
    ln_mix_g = 1.0 + 0.02 * nrm(ks[9], (DEPTH, D_MODEL), jnp.float32)
    ln_mix_b = 0.02 * nrm(ks[10], (DEPTH, D_MODEL), jnp.float32)
    ffn_w_up = nrm(ks[11], (DEPTH, D_MODEL, 2 * FFN_DIM), jnp.float32) * s_in
    ffn_w_down = nrm(ks[12], (DEPTH, FFN_DIM, D_MODEL), jnp.float32) * (FFN_DIM ** -0.5 * BETA)
    ln_ffn_g = 1.0 + 0.02 * nrm(ks[13], (DEPTH, D_MODEL), jnp.float32)
    ln_ffn_b = 0.02 * nrm(ks[14], (DEPTH, D_MODEL), jnp.float32)
    return {'x': x, 'mlstm_w_in': mlstm_w_in, 'mlstm_b_gate': mlstm_b_gate,
            'mlstm_norm_g': mlstm_norm_g, 'mlstm_w_out': mlstm_w_out,
            'fox_w_in': fox_w_in, 'fox_b_f': fox_b_f, 'fox_w_out': fox_w_out,
            'ln_mix_g': ln_mix_g, 'ln_mix_b': ln_mix_b,
            'ffn_w_up': ffn_w_up, 'ffn_w_down': ffn_w_down,
            'ln_ffn_g': ln_ffn_g, 'ln_ffn_b': ln_ffn_b}


def reference(x, mlstm_w_in, mlstm_b_gate, mlstm_norm_g, mlstm_w_out,
              fox_w_in, fox_b_f, fox_w_out, ln_mix_g, ln_mix_b,
              ffn_w_up, ffn_w_down, ln_ffn_g, ln_ffn_b):
    h = x
    for layer in range(DEPTH):
        slot = layer // N_MIXERS
        if layer % N_MIXERS == 0:
            mixed = mlstm_mixer(h, mlstm_w_in[slot], mlstm_b_gate[slot], mlstm_norm_g[slot], mlstm_w_out[slot])
        else:
            mixed = fox_mixer(h, fox_w_in[slot], fox_b_f[slot], fox_w_out[slot])
        h = layer_norm(ALPHA * h + mixed, ln_mix_g[layer], ln_mix_b[layer])
        h = layer_norm(ALPHA * h + swiglu(h, ffn_w_up[layer], ffn_w_down[layer]), ln_ffn_g[layer], ln_ffn_b[layer])
    return h
```

```python
import functools

import jax
import jax.numpy as jnp
from jax import lax
from jax.experimental import pallas as pl
from jax.experimental.pallas import tpu as pltpu

F32 = jnp.float32
BF16 = jnp.bfloat16

MLSTM_HEADS = 8
FOX_HEAD_DIM = 128
GATE_SOFT_CAP = 15.0
HEAD_NORM_EPS = 1e-6
LN_EPS = 1e-5
N_MIXERS = 2

LANES = 128
SUBLANES = 8
NEG_BIG = -1e30

MM_TM = 1024
MM_TN = 512
LN_TM = 512
LN_TK = 512
LN_NCHUNK = 1024
LN_ROWS = 32
GATE_TM = 512
FFN_PAD = 1024
MLSTM_CHUNK = 256
FOX_TQ = 512
FOX_TS = 512
VMEM_LIMIT = 56 * 1024 * 1024


def _tile(n, pref):
    if n <= pref:
        return n
    t = (pref // LANES) * LANES
    while t >= LANES:
        if n % t == 0:
            return t
        t -= LANES
    raise ValueError(f"no lane-aligned tile of {n} below {pref}")


def _params(*sem):
    return pltpu.CompilerParams(dimension_semantics=sem, vmem_limit_bytes=VMEM_LIMIT)


def _dot(a, b):
    return jnp.dot(a, b, preferred_element_type=F32)


def _dot_nt(a, b):
    return lax.dot_general(a, b, (((1,), (1,)), ((), ())), preferred_element_type=F32)


def _dot_tn(a, b):
    return lax.dot_general(a, b, (((0,), (0,)), ((), ())), preferred_element_type=F32)


def _split3(x):
    hi = x.astype(BF16)
    r1 = x - hi.astype(F32)
    mid = r1.astype(BF16)
    lo = (r1 - mid.astype(F32)).astype(BF16)
    return hi, mid, lo


def _sigmoid(x):
    return 1.0 / (1.0 + jnp.exp(-x))


def _log_sigmoid(x):
    return jnp.minimum(x, 0.0) - jnp.log1p(jnp.exp(-jnp.abs(x)))


def _soft_cap(z):
    return GATE_SOFT_CAP * jnp.tanh(z / GATE_SOFT_CAP)


def _mm_kernel(a_ref, b_ref, o_ref):
    o_ref[...] = _dot(a_ref[...], b_ref[...]).astype(o_ref.dtype)


def _matmul(a, b, out_dtype):
    m, k = a.shape
    n = b.shape[1]
    tm, tn = _tile(m, MM_TM), _tile(n, MM_TN)
    return pl.pallas_call(
        _mm_kernel,
        grid=(m // tm, n // tn),
        in_specs=[pl.BlockSpec((tm, k), lambda i, j: (i, 0)),
                  pl.BlockSpec((k, tn), lambda i, j: (0, j))],
        out_specs=pl.BlockSpec((tm, tn), lambda i, j: (i, j)),
        out_shape=jax.ShapeDtypeStruct((m, n), out_dtype),
        compiler_params=_params("parallel", "arbitrary"),
        name="proj",
    )(a, b)


def _gate_kernel(h_ref, w_ref, b_ref, o_ref):
    x = h_ref[...]
    w = w_ref[...]
    x_hi = x.astype(BF16)
    x_lo = (x - x_hi.astype(F32)).astype(BF16)
    w_hi = w.astype(BF16)
    w_lo = (w - w_hi.astype(F32)).astype(BF16)
    acc = _dot(x_hi, w_hi) + _dot(x_lo, w_hi) + _dot(x_hi, w_lo)
    o_ref[...] = acc + b_ref[...]


def _gate_proj(h32, w_gate, bias):
    m, d = h32.shape
    g = w_gate.shape[1]
    w_pad = jnp.pad(w_gate, ((0, 0), (0, LANES - g)))
    b_pad = jnp.pad(bias.astype(F32), (0, LANES - g)).reshape(1, LANES)
    tm = _tile(m, GATE_TM)
    return pl.pallas_call(
        _gate_kernel,
        grid=(m // tm,),
        in_specs=[pl.BlockSpec((tm, d), lambda i: (i, 0)),
                  pl.BlockSpec((d, LANES), lambda i: (0, 0)),
                  pl.BlockSpec((1, LANES), lambda i: (0, 0))],
        out_specs=pl.BlockSpec((tm, LANES), lambda i: (i, 0)),
        out_shape=jax.ShapeDtypeStruct((m, LANES), F32),
        compiler_params=_params("parallel"),
        name="gate_proj",
    )(h32, w_pad, b_pad)


def _mm_res_ln_kernel(a_ref, b_ref, h_ref, g_ref, beta_ref, o32_ref, obf_ref, *, alpha, nk):
    k = pl.program_id(1)

    @pl.when(k == 0)
    def _():
        o32_ref[...] = alpha * h_ref[...]

    a = a_ref[...]
    tm, d = o32_ref.shape
    nch = min(d, LN_NCHUNK)
    for n0 in range(0, d, nch):
        o32_ref[:, n0:n0 + nch] += _dot(a, b_ref[:, n0:n0 + nch])

    @pl.when(k == nk - 1)
    def _():
        rb = min(tm, LN_ROWS)

        def rows(r, carry):
            r0 = pl.multiple_of(r * rb, rb)
            y = o32_ref[pl.ds(r0, rb), :]
            mu = jnp.mean(y, axis=-1, keepdims=True)
            yc = y - mu
            var = jnp.mean(yc * yc, axis=-1, keepdims=True)
            out = yc * lax.rsqrt(var + LN_EPS) * g_ref[...] + beta_ref[...]
            o32_ref[pl.ds(r0, rb), :] = out
            obf_ref[pl.ds(r0, rb), :] = out.astype(BF16)
            return carry

        lax.fori_loop(0, tm // rb, rows, 0)


def _mm_res_ln(a, b, h32, g, beta, alpha):
    m, kdim = a.shape
    d = b.shape[1]
    tm, tk = _tile(m, LN_TM), _tile(kdim, LN_TK)
    nk = kdim // tk
    return pl.pallas_call(
        functools.partial(_mm_res_ln_kernel, alpha=alpha, nk=nk),
        grid=(m // tm, nk),
        in_specs=[pl.BlockSpec((tm, tk), lambda i, k: (i, k)),
                  pl.BlockSpec((tk, d), lambda i, k: (k, 0)),
                  pl.BlockSpec((tm, d), lambda i, k: (i, 0)),
                  pl.BlockSpec((1, d), lambda i, k: (0, 0)),
                  pl.BlockSpec((1, d), lambda i, k: (0, 0))],
        out_specs=[pl.BlockSpec((tm, d), lambda i, k: (i, 0)),
                   pl.BlockSpec((tm, d), lambda i, k: (i, 0))],
        out_shape=[jax.ShapeDtypeStruct((m, d), F32), jax.ShapeDtypeStruct((m, d), BF16)],
        compiler_params=_params("parallel", "arbitrary"),
        name="proj_res_ln",
    )(a, b, h32, g.reshape(1, d).astype(F32), beta.reshape(1, d).astype(F32))


def _ffn_up_kernel(x_ref, wg_ref, wu_ref, o_ref):
    x = x_ref[...]
    gate = _dot(x, wg_ref[...])
    up = _dot(x, wu_ref[...])
    o_ref[...] = (gate * _sigmoid(gate) * up).astype(o_ref.dtype)


def _ffn_up(x, w_up_pad):
    m, d = x.shape
    fp = w_up_pad.shape[1] // 2
    tm, tn = _tile(m, MM_TM), _tile(fp, MM_TN)
    nj = fp // tn
    return pl.pallas_call(
        _ffn_up_kernel,
        grid=(m // tm, nj),
        in_specs=[pl.BlockSpec((tm, d), lambda i, j: (i, 0)),
                  pl.BlockSpec((d, tn), lambda i, j: (0, j)),
                  pl.BlockSpec((d, tn), lambda i, j: (0, nj + j))],
        out_specs=pl.BlockSpec((tm, tn), lambda i, j: (i, j)),
        out_shape=jax.ShapeDtypeStruct((m, fp), BF16),
        compiler_params=_params("parallel", "arbitrary"),
        name="ffn_up",
    )(x, w_up_pad, w_up_pad)


def _mlstm_kernel(q_ref, k_ref, v_ref, o_ref, gc_ref, gri_ref, grf_ref, ng_ref, out_ref,
                  ct_ref, m_ref, *, chunk, dk, dv, heads):
    h = pl.program_id(1)
    c = pl.program_id(2)
    L = chunk

    @pl.when(c == 0)
    def _():
        ct_ref[...] = jnp.zeros_like(ct_ref)
        m_ref[...] = jnp.zeros_like(m_ref)

    gc = gc_ref[...]
    lane = lax.broadcasted_iota(jnp.int32, (L, LANES), 1)
    ig_c = jnp.sum(jnp.where(lane == h, gc, 0.0), axis=1, keepdims=True)
    fg_c = jnp.sum(jnp.where(lane == heads + h, gc, 0.0), axis=1, keepdims=True)
    ig_r = gri_ref[pl.ds(h, 1), :]
    fg_r = grf_ref[pl.ds(h, 1), :]
    ig_c, ig_r = _soft_cap(ig_c), _soft_cap(ig_r)
    lf_c, lf_r = _log_sigmoid(_soft_cap(fg_c)), _log_sigmoid(_soft_cap(fg_r))

    ti = lax.broadcasted_iota(jnp.int32, (L, L), 0)
    si = lax.broadcasted_iota(jnp.int32, (L, L), 1)
    causal = si <= ti
    tri_lo = causal.astype(BF16)
    tri_up = (ti <= si).astype(BF16)
    b_c = sum(_dot(tri_lo, p) for p in _split3(jnp.broadcast_to(lf_c, (L, LANES))))[:, :1]
    b_r = sum(_dot(p, tri_up) for p in _split3(jnp.broadcast_to(lf_r, (SUBLANES, L))))[:1, :]

    m_prev = m_ref[...]
    d_log = jnp.where(causal, b_c - b_r + ig_r, NEG_BIG)
    g_log = b_c + m_prev
    m_t = jnp.maximum(g_log, jnp.max(d_log, axis=1, keepdims=True))
    w_intra = jnp.exp(d_log - m_t)
    w_inter = jnp.exp(g_log - m_t)

    q = q_ref[...]
    k = k_ref[...]
    v = v_ref[...]
    scale = dk ** -0.5
    s = _dot_nt(q, k) * scale * w_intra
    ct = ct_ref[...]
    inter = _dot(q, ct.astype(BF16)) * scale
    num = _dot(s.astype(BF16), v) + w_inter * inter[:, :dv]
    den = jnp.sum(s, axis=1, keepdims=True) + w_inter * inter[:, dv:dv + 1]
    hh = num * (1.0 / jnp.maximum(jnp.abs(den), jnp.exp(-m_t)))

    b_end = b_c[L - 1:L, :]
    a_c = b_end - b_c + ig_c
    a_r = b_end - b_r + ig_r
    m_new = jnp.maximum(b_end + m_prev, jnp.max(a_r, axis=1, keepdims=True))
    decay = jnp.exp(b_end + m_prev - m_new)
    wa_c = jnp.exp(a_c - m_new)
    one_lane = (lax.broadcasted_iota(jnp.int32, (L, LANES), 1) == 0).astype(F32)
    wv = jnp.concatenate([wa_c * v.astype(F32), wa_c * one_lane], axis=1).astype(BF16)
    ct_ref[...] = decay * ct + _dot_tn(k, wv)
    m_ref[...] = m_new

    hn = hh * lax.rsqrt(jnp.mean(hh * hh, axis=1, keepdims=True) + HEAD_NORM_EPS)
    out_ref[...] = (hn * ng_ref[...] * _sigmoid(o_ref[...])).astype(out_ref.dtype)


def _mlstm_core(qkv, o, gates, gates_t, norm_g, batch, seq):
    heads = MLSTM_HEADS
    assert heads == SUBLANES, "row-form gate blocks assume one sublane tile per gate kind"
    m = qkv.shape[0]
    hv = o.shape[1]
    dv = hv // heads
    dk = (qkv.shape[1] - hv) // (2 * heads)
    assert 2 * heads * dk == hv, "v blocks are addressed in units of dv"
    chunk = _tile(seq, MLSTM_CHUNK)
    nc = seq // chunk
    row = lambda b, h, c: b * nc + c
    return pl.pallas_call(
        functools.partial(_mlstm_kernel, chunk=chunk, dk=dk, dv=dv, heads=heads),
        grid=(batch, heads, nc),
        in_specs=[pl.BlockSpec((chunk, dk), lambda b, h, c: (row(b, h, c), h)),
                  pl.BlockSpec((chunk, dk), lambda b, h, c: (row(b, h, c), heads + h)),
                  pl.BlockSpec((chunk, dv), lambda b, h, c: (row(b, h, c), heads + h)),
                  pl.BlockSpec((chunk, dv), lambda b, h, c: (row(b, h, c), h)),
                  pl.BlockSpec((chunk, LANES), lambda b, h, c: (row(b, h, c), 0)),
                  pl.BlockSpec((SUBLANES, chunk), lambda b, h, c: (0, row(b, h, c))),
                  pl.BlockSpec((SUBLANES, chunk), lambda b, h, c: (1, row(b, h, c))),
                  pl.BlockSpec((1, dv), lambda b, h, c: (0, h))],
        out_specs=pl.BlockSpec((chunk, dv), lambda b, h, c: (row(b, h, c), h)),
        out_shape=jax.ShapeDtypeStruct((m, hv), BF16),
        scratch_shapes=[pltpu.VMEM((dk, dv + LANES), F32), pltpu.VMEM((1, 1), F32)],
        compiler_params=_params("parallel", "parallel", "arbitrary"),
        name="mlstm_core",
    )(qkv, qkv, qkv, o, gates, gates_t, gates_t, norm_g.reshape(1, hv).astype(F32))


def _fox_cum_kernel(g_ref, col_ref, row_ref, carry_ref, *, ts):
    @pl.when(pl.program_id(1) == 0)
    def _():
        carry_ref[...] = jnp.zeros_like(carry_ref)

    logf = _log_sigmoid(g_ref[...])
    ti = lax.broadcasted_iota(jnp.int32, (ts, ts), 0)
    si = lax.broadcasted_iota(jnp.int32, (ts, ts), 1)
    tri_lo = (si <= ti).astype(BF16)
    cum = sum(_dot(tri_lo, p) for p in _split3(logf)) + carry_ref[...]
    carry_ref[...] = cum[ts - 1:ts, :]
    col_ref[...] = cum
    row_ref[...] = cum.T


def _fox_cumsum(gates, batch, seq):
    ts = _tile(seq, FOX_TS)
    ns = seq // ts
    return pl.pallas_call(
        functools.partial(_fox_cum_kernel, ts=ts),
        grid=(batch, ns),
        in_specs=[pl.BlockSpec((ts, LANES), lambda b, s: (b * ns + s, 0))],
        out_specs=[pl.BlockSpec((ts, LANES), lambda b, s: (b * ns + s, 0)),
                   pl.BlockSpec((LANES, ts), lambda b, s: (b, s))],
        out_shape=[jax.ShapeDtypeStruct((batch * seq, LANES), F32),
                   jax.ShapeDtypeStruct((batch * LANES, seq), F32)],
        scratch_shapes=[pltpu.VMEM((1, LANES), F32)],
        compiler_params=_params("parallel", "arbitrary"),
        name="fox_cumsum",
    )(gates)


def _fox_attn_kernel(q_ref, k_ref, v_ref, cc_ref, cr_ref, o_ref, m_scr, l_scr, acc_scr, *, tq):
    h = pl.program_id(1)
    i = pl.program_id(2)
    q = q_ref[...]
    lane = lax.broadcasted_iota(jnp.int32, (tq, LANES), 1)
    cq = jnp.sum(jnp.where(lane == h, cc_ref[...], 0.0), axis=1, keepdims=True)
    r = h % SUBLANES
    scale = FOX_HEAD_DIM ** -0.5

    m_scr[...] = jnp.full_like(m_scr, NEG_BIG)
    l_scr[...] = jnp.zeros_like(l_scr)
    acc_scr[...] = jnp.zeros_like(acc_scr)

    def block(j, on_diagonal):
        start = pl.multiple_of(j * tq, tq)
        kb = k_ref[pl.ds(start, tq), :]
        vb = v_ref[pl.ds(start, tq), :]
        ck = cr_ref[pl.ds(r, 1), pl.ds(start, tq)]
        s = _dot_nt(q, kb) * scale + (cq - ck)
        if on_diagonal:
            ti = lax.broadcasted_iota(jnp.int32, (tq, tq), 0)
            si = lax.broadcasted_iota(jnp.int32, (tq, tq), 1)
            s = jnp.where(si <= ti, s, NEG_BIG)
        m_old = m_scr[...]
        m_new = jnp.maximum(m_old, jnp.max(s, axis=1, keepdims=True))
        p = jnp.exp(s - m_new)
        a = jnp.exp(m_old - m_new)
        l_scr[...] = a * l_scr[...] + jnp.sum(p, axis=1, keepdims=True)
        acc_scr[...] = a * acc_scr[...] + _dot(p.astype(BF16), vb)
        m_scr[...] = m_new

    block(i, True)

    def body(j, carry):
        block(j, False)
        return carry

    lax.fori_loop(0, i, body, 0)
    o_ref[...] = (acc_scr[...] * (1.0 / l_scr[...])).astype(o_ref.dtype)


def _fox_attn(qkv, cum_col, cum_row, batch, seq):
    dh = FOX_HEAD_DIM
    assert dh == LANES
    m = qkv.shape[0]
    heads = qkv.shape[1] // (3 * dh)
    assert heads <= LANES
    tq = _tile(seq, FOX_TQ)
    nq = seq // tq
    row_blocks = LANES // SUBLANES
    return pl.pallas_call(
        functools.partial(_fox_attn_kernel, tq=tq),
        grid=(batch, heads, nq),
        in_specs=[pl.BlockSpec((tq, dh), lambda b, h, i: (b * nq + i, h)),
                  pl.BlockSpec((seq, dh), lambda b, h, i: (b, heads + h)),
                  pl.BlockSpec((seq, dh), lambda b, h, i: (b, 2 * heads + h)),
                  pl.BlockSpec((tq, LANES), lambda b, h, i: (b * nq + i, 0)),
                  pl.BlockSpec((SUBLANES, seq), lambda b, h, i: (b * row_blocks + h // SUBLANES, 0))],
        out_specs=pl.BlockSpec((tq, dh), lambda b, h, i: (b * nq + i, h)),
        out_shape=jax.ShapeDtypeStruct((m, heads * dh), BF16),
        scratch_shapes=[pltpu.VMEM((tq, 1), F32), pltpu.VMEM((tq, 1), F32),
                        pltpu.VMEM((tq, dh), F32)],
        compiler_params=_params("parallel", "parallel", "arbitrary"),
        name="fox_attn",
    )(qkv, qkv, qkv, cum_col, cum_row)


def _pad_cols(w, width):
    return jnp.pad(w, ((0, 0), (0, width - w.shape[1])))


def kernel(x, mlstm_w_in, mlstm_b_gate, mlstm_norm_g, mlstm_w_out, fox_w_in, fox_b_f, fox_w_out,
           ln_mix_g, ln_mix_b, ffn_w_up, ffn_w_down, ln_ffn_g, ln_ffn_b):
    batch, seq, d = x.shape
    depth = ln_mix_g.shape[0]
    alpha = (2 * depth) ** 0.25
    m = batch * seq
    ffn = ffn_w_down.shape[1]
    ffn_pad = -(-ffn // FFN_PAD) * FFN_PAD
    mlstm_main = mlstm_w_in.shape[2] - 2 * MLSTM_HEADS
    hv = mlstm_w_out.shape[1]
    fox_main = fox_w_out.shape[1] * 3

    h32 = x.reshape(m, d)
    hbf = h32.astype(BF16)
    for layer in range(depth):
        slot = layer // N_MIXERS
        if layer % N_MIXERS == 0:
            w = mlstm_w_in[slot]
            qkv = _matmul(hbf, w[:, :mlstm_main - hv].astype(BF16), BF16)
            o = _matmul(hbf, w[:, mlstm_main - hv:mlstm_main].astype(BF16), F32)
            gates = _gate_proj(h32, w[:, mlstm_main:], mlstm_b_gate[slot])
            mixed = _mlstm_core(qkv, o, gates, gates.T, mlstm_norm_g[slot], batch, seq)
            w_out = mlstm_w_out[slot].astype(BF16)
        else:
            w = fox_w_in[slot]
            qkv = _matmul(hbf, w[:, :fox_main].astype(BF16), BF16)
            gates = _gate_proj(h32, w[:, fox_main:], fox_b_f[slot])
            cum_col, cum_row = _fox_cumsum(gates, batch, seq)
            mixed = _fox_attn(qkv, cum_col, cum_row, batch, seq)
            w_out = fox_w_out[slot].astype(BF16)
        h32, hbf = _mm_res_ln(mixed, w_out, h32, ln_mix_g[layer], ln_mix_b[layer], alpha)

        w_up = ffn_w_up[layer]
        w_up_pad = jnp.concatenate([_pad_cols(w_up[:, :ffn], ffn_pad),
                                    _pad_cols(w_up[:, ffn:], ffn_pad)], axis=1).astype(BF16)
        w_down_pad = jnp.pad(ffn_w_down[layer], ((0, ffn_pad - ffn), (0, 0))).astype(BF16)
        hidden = _ffn_up(hbf, w_up_pad)
        h32, hbf = _mm_res_ln(hidden, w_down_pad, h32, ln_ffn_g[layer], ln_ffn_b[layer], alpha)
    return h32.reshape(batch, seq, d)
```

```python
import functools
import math

import jax
import jax.numpy as jnp
from jax import lax
from jax.experimental import pallas as pl
from jax.experimental.pallas import tpu as pltpu

F32 = jnp.float32
BF16 = jnp.bfloat16

MLSTM_HEADS = 8
FOX_HEAD_DIM = 128
GATE_SOFT_CAP = 15.0
HEAD_NORM_EPS = 1e-6
LN_EPS = 1e-5
N_MIXERS = 2

LANES = 128
SUBLANES = 8
NEG_BIG = -1e30
LOG2E = math.log2(math.e)

MM_TM = 1024
MM_TN = 512
FFN_TN = 256
LN_TM = 512
LN_TK = 512
LN_NCHUNK = 1024
LN_ROWS = 32
GATE_TM = 512
CAST_ROWS = 256
FFN_PAD = 1024
MLSTM_CHUNK = 256
FOX_TQ = 512
FOX_TS = 512
VMEM_LIMIT = 56 * 1024 * 1024


def _tile(n, pref):
    if n <= pref:
        return n
    t = (pref // LANES) * LANES
    while t >= LANES:
        if n % t == 0:
            return t
        t -= LANES
    raise ValueError(f"no lane-aligned tile of {n} below {pref}")


def _params(*sem):
    return pltpu.CompilerParams(dimension_semantics=sem, vmem_limit_bytes=VMEM_LIMIT)


def _dot(a, b):
    return jnp.dot(a, b, preferred_element_type=F32)


def _dot_nt(a, b):
    return lax.dot_general(a, b, (((1,), (1,)), ((), ())), preferred_element_type=F32)


def _dot_tn(a, b):
    return lax.dot_general(a, b, (((0,), (0,)), ((), ())), preferred_element_type=F32)


def _split3(x):
    hi = x.astype(BF16)
    r1 = x - hi.astype(F32)
    mid = r1.astype(BF16)
    lo = (r1 - mid.astype(F32)).astype(BF16)
    return hi, mid, lo


def _sigmoid(x):
    return 1.0 / (1.0 + jnp.exp(-x))


def _log_sigmoid(x):
    return jnp.minimum(x, 0.0) - jnp.log1p(jnp.exp(-jnp.abs(x)))


def _soft_cap(z):
    return GATE_SOFT_CAP * jnp.tanh(z / GATE_SOFT_CAP)


def _cast_kernel(w_ref, o_ref, *, n_real):
    r = pl.program_id(0)

    @pl.when(r < n_real)
    def _():
        o_ref[...] = w_ref[...].astype(BF16)

    @pl.when(r >= n_real)
    def _():
        o_ref[...] = jnp.zeros_like(o_ref)


def _cast_pad_rows(w_stack, idx, rows_pad):
    _, rows, cols = w_stack.shape
    tr = _tile(rows, CAST_ROWS)
    assert rows_pad % tr == 0
    n_real = rows // tr
    return pl.pallas_call(
        functools.partial(_cast_kernel, n_real=n_real),
        grid=(rows_pad // tr,),
        in_specs=[pl.BlockSpec((None, tr, cols), lambda r: (idx, jnp.minimum(r, n_real - 1), 0))],
        out_specs=pl.BlockSpec((tr, cols), lambda r: (r, 0)),
        out_shape=jax.ShapeDtypeStruct((rows_pad, cols), BF16),
        compiler_params=_params("parallel"),
        name="cast_w",
    )(w_stack)


def _proj_kernel(x_ref, w_ref, o_ref, wbf_ref, *, n_scaled, scale):
    j = pl.program_id(0)

    @pl.when(pl.program_id(1) == 0)
    def _():
        wbf_ref[...] = w_ref[...].astype(BF16)

    acc = _dot(x_ref[...], wbf_ref[...])
    if n_scaled:
        acc = acc * jnp.where(j < n_scaled, scale, 1.0)
    o_ref[...] = acc.astype(o_ref.dtype)


def _proj(x, w_stack, idx, col0, ncols, out_dtype, scaled_cols=0, scale=1.0):
    m, k = x.shape
    tm, tn = _tile(m, MM_TM), _tile(ncols, MM_TN)
    assert col0 % tn == 0 and scaled_cols % tn == 0
    j0 = col0 // tn
    return pl.pallas_call(
        functools.partial(_proj_kernel, n_scaled=scaled_cols // tn, scale=scale),
        grid=(ncols // tn, m // tm),
        in_specs=[pl.BlockSpec((tm, k), lambda j, i: (i, 0)),
                  pl.BlockSpec((None, k, tn), lambda j, i: (idx, 0, j0 + j))],
        out_specs=pl.BlockSpec((tm, tn), lambda j, i: (i, j)),
        out_shape=jax.ShapeDtypeStruct((m, ncols), out_dtype),
        scratch_shapes=[pltpu.VMEM((k, tn), BF16)],
        compiler_params=_params("arbitrary", "arbitrary"),
        name="proj",
    )(x, w_stack)


def _gate_kernel(h_ref, w_ref, b_ref, o_ref):
    x = h_ref[...]
    w = w_ref[...]
    x_hi = x.astype(BF16)
    x_lo = (x - x_hi.astype(F32)).astype(BF16)
    w_hi = w.astype(BF16)
    w_lo = (w - w_hi.astype(F32)).astype(BF16)
    acc = _dot(x_hi, w_hi) + _dot(x_lo, w_hi) + _dot(x_hi, w_lo)
    o_ref[...] = acc + b_ref[...]


def _gate_proj(h32, w_gate, bias):
    m, d = h32.shape
    g = w_gate.shape[1]
    w_pad = jnp.pad(w_gate, ((0, 0), (0, LANES - g)))
    b_pad = jnp.pad(bias.astype(F32), (0, LANES - g)).reshape(1, LANES)
    tm = _tile(m, GATE_TM)
    return pl.pallas_call(
        _gate_kernel,
        grid=(m // tm,),
        in_specs=[pl.BlockSpec((tm, d), lambda i: (i, 0)),
                  pl.BlockSpec((d, LANES), lambda i: (0, 0)),
                  pl.BlockSpec((1, LANES), lambda i: (0, 0))],
        out_specs=pl.BlockSpec((tm, LANES), lambda i: (i, 0)),
        out_shape=jax.ShapeDtypeStruct((m, LANES), F32),
        compiler_params=_params("parallel"),
        name="gate_proj",
    )(h32, w_pad, b_pad)


def _mm_res_ln_kernel(a_ref, b_ref, h_ref, g_ref, beta_ref, o32_ref, obf_ref, *, alpha, nk):
    k = pl.program_id(1)

    @pl.when(k == 0)
    def _():
        o32_ref[...] = alpha * h_ref[...]

    a = a_ref[...]
    tm, d = o32_ref.shape
    nch = min(d, LN_NCHUNK)
    for n0 in range(0, d, nch):
        o32_ref[:, n0:n0 + nch] += _dot(a, b_ref[:, n0:n0 + nch])

    @pl.when(k == nk - 1)
    def _():
        rb = min(tm, LN_ROWS)

        def rows(r, carry):
            r0 = pl.multiple_of(r * rb, rb)
            y = o32_ref[pl.ds(r0, rb), :]
            mu = jnp.mean(y, axis=-1, keepdims=True)
            yc = y - mu
            var = jnp.mean(yc * yc, axis=-1, keepdims=True)
            out = yc * lax.rsqrt(var + LN_EPS) * g_ref[...] + beta_ref[...]
            o32_ref[pl.ds(r0, rb), :] = out
            obf_ref[pl.ds(r0, rb), :] = out.astype(BF16)
            return carry

        lax.fori_loop(0, tm // rb, rows, 0)


def _mm_res_ln(a, b, h32, g, beta, alpha):
    m, kdim = a.shape
    d = b.shape[1]
    tm, tk = _tile(m, LN_TM), _tile(kdim, LN_TK)
    nk = kdim // tk
    return pl.pallas_call(
        functools.partial(_mm_res_ln_kernel, alpha=alpha, nk=nk),
        grid=(m // tm, nk),
        in_specs=[pl.BlockSpec((tm, tk), lambda i, k: (i, k)),
                  pl.BlockSpec((tk, d), lambda i, k: (k, 0)),
                  pl.BlockSpec((tm, d), lambda i, k: (i, 0)),
                  pl.BlockSpec((1, d), lambda i, k: (0, 0)),
                  pl.BlockSpec((1, d), lambda i, k: (0, 0))],
        out_specs=[pl.BlockSpec((tm, d), lambda i, k: (i, 0)),
                   pl.BlockSpec((tm, d), lambda i, k: (i, 0))],
        out_shape=[jax.ShapeDtypeStruct((m, d), F32), jax.ShapeDtypeStruct((m, d), BF16)],
        compiler_params=_params("parallel", "arbitrary"),
        name="proj_res_ln",
    )(a, b, h32, g.reshape(1, d).astype(F32), beta.reshape(1, d).astype(F32))


def _ffn_up_kernel(x_ref, wg_ref, wu_ref, o_ref, wg_bf, wu_bf, *, n_real):
    j = pl.program_id(0)

    @pl.when((pl.program_id(1) == 0) & (j < n_real))
    def _():
        wg_bf[...] = wg_ref[...].astype(BF16)
        wu_bf[...] = wu_ref[...].astype(BF16)

    @pl.when(j < n_real)
    def _():
        x = x_ref[...]
        gate = _dot(x, wg_bf[...])
        up = _dot(x, wu_bf[...])
        o_ref[...] = (gate * _sigmoid(gate) * up).astype(o_ref.dtype)

    @pl.when(j >= n_real)
    def _():
        o_ref[...] = jnp.zeros_like(o_ref)


def _ffn_up(x, w_up_stack, idx, ffn_pad):
    m, d = x.shape
    f = w_up_stack.shape[2] // 2
    tm, tn = _tile(m, MM_TM), _tile(f, FFN_TN)
    assert ffn_pad % tn == 0
    n_real = f // tn
    real = lambda j: jnp.minimum(j, n_real - 1)
    return pl.pallas_call(
        functools.partial(_ffn_up_kernel, n_real=n_real),
        grid=(ffn_pad // tn, m // tm),
        in_specs=[pl.BlockSpec((tm, d), lambda j, i: (i, 0)),
                  pl.BlockSpec((None, d, tn), lambda j, i: (idx, 0, real(j))),
                  pl.BlockSpec((None, d, tn), lambda j, i: (idx, 0, n_real + real(j)))],
        out_specs=pl.BlockSpec((tm, tn), lambda j, i: (i, j)),
        out_shape=jax.ShapeDtypeStruct((m, ffn_pad), BF16),
        scratch_shapes=[pltpu.VMEM((d, tn), BF16), pltpu.VMEM((d, tn), BF16)],
        compiler_params=_params("arbitrary", "arbitrary"),
        name="ffn_up",
    )(x, w_up_stack, w_up_stack)


def _mlstm_kernel(q_ref, k_ref, v_ref, o_ref, gc_ref, gri_ref, grf_ref, ng_ref, out_ref,
                  ct_ref, m_ref, *, chunk, dv, heads):
    h = pl.program_id(1)
    c = pl.program_id(2)
    L = chunk

    @pl.when(c == 0)
    def _():
        ct_ref[...] = jnp.zeros_like(ct_ref)
        m_ref[...] = jnp.zeros_like(m_ref)

    gc = gc_ref[...]
    lane = lax.broadcasted_iota(jnp.int32, (L, LANES), 1)
    ig_c = jnp.sum(jnp.where(lane == h, gc, 0.0), axis=1, keepdims=True)
    fg_c = jnp.sum(jnp.where(lane == heads + h, gc, 0.0), axis=1, keepdims=True)
    ig_r = gri_ref[pl.ds(h, 1), :]
    fg_r = grf_ref[pl.ds(h, 1), :]
    ig_c, ig_r = _soft_cap(ig_c), _soft_cap(ig_r)
    lf_c, lf_r = _log_sigmoid(_soft_cap(fg_c)), _log_sigmoid(_soft_cap(fg_r))

    ti = lax.broadcasted_iota(jnp.int32, (L, L), 0)
    si = lax.broadcasted_iota(jnp.int32, (L, L), 1)
    causal = si <= ti
    tri_lo = causal.astype(BF16)
    tri_up = (ti <= si).astype(BF16)
    b_c = sum(_dot(tri_lo, p) for p in _split3(jnp.broadcast_to(lf_c, (L, LANES))))[:, :1]
    b_r = sum(_dot(p, tri_up) for p in _split3(jnp.broadcast_to(lf_r, (SUBLANES, L))))[:1, :]

    m_prev = m_ref[...]
    d_log = jnp.where(causal, b_c - b_r + ig_r, NEG_BIG)
    g_log = b_c + m_prev
    m_t = jnp.maximum(g_log, jnp.max(d_log, axis=1, keepdims=True))
    w_intra = jnp.exp(d_log - m_t)
    w_inter = jnp.exp(g_log - m_t)

    q = q_ref[...]
    k = k_ref[...]
    v = v_ref[...]
    s = _dot_nt(q, k) * w_intra
    ct = ct_ref[...]
    inter = _dot(q, ct.astype(BF16))
    num = _dot(s.astype(BF16), v) + w_inter * inter[:, :dv]
    den = jnp.sum(s, axis=1, keepdims=True) + w_inter * inter[:, dv:dv + 1]
    hh = num * (1.0 / jnp.maximum(jnp.abs(den), jnp.exp(-m_t)))

    b_end = b_c[L - 1:L, :]
    a_c = b_end - b_c + ig_c
    a_r = b_end - b_r + ig_r
    m_new = jnp.maximum(b_end + m_prev, jnp.max(a_r, axis=1, keepdims=True))
    decay = jnp.exp(b_end + m_prev - m_new)
    wa_c = jnp.exp(a_c - m_new)
    one_lane = (lax.broadcasted_iota(jnp.int32, (L, LANES), 1) == 0).astype(F32)
    wv = jnp.concatenate([wa_c * v.astype(F32), wa_c * one_lane], axis=1).astype(BF16)
    ct_ref[...] = decay * ct + _dot_tn(k, wv)
    m_ref[...] = m_new

    hn = hh * lax.rsqrt(jnp.mean(hh * hh, axis=1, keepdims=True) + HEAD_NORM_EPS)
    out_ref[...] = (hn * ng_ref[...] * _sigmoid(o_ref[...])).astype(out_ref.dtype)


def _mlstm_core(qkv, o, gates, gates_t, norm_g, batch, seq):
    heads = MLSTM_HEADS
    assert heads == SUBLANES, "row-form gate blocks assume one sublane tile per gate kind"
    m = qkv.shape[0]
    hv = o.shape[1]
    dv = hv // heads
    dk = (qkv.shape[1] - hv) // (2 * heads)
    assert 2 * heads * dk == hv, "v blocks are addressed in units of dv"
    chunk = _tile(seq, MLSTM_CHUNK)
    nc = seq // chunk
    row = lambda b, h, c: b * nc + c
    return pl.pallas_call(
        functools.partial(_mlstm_kernel, chunk=chunk, dv=dv, heads=heads),
        grid=(batch, heads, nc),
        in_specs=[pl.BlockSpec((chunk, dk), lambda b, h, c: (row(b, h, c), h)),
                  pl.BlockSpec((chunk, dk), lambda b, h, c: (row(b, h, c), heads + h)),
                  pl.BlockSpec((chunk, dv), lambda b, h, c: (row(b, h, c), heads + h)),
                  pl.BlockSpec((chunk, dv), lambda b, h, c: (row(b, h, c), h)),
                  pl.BlockSpec((chunk, LANES), lambda b, h, c: (row(b, h, c), 0)),
                  pl.BlockSpec((SUBLANES, chunk), lambda b, h, c: (0, row(b, h, c))),
                  pl.BlockSpec((SUBLANES, chunk), lambda b, h, c: (1, row(b, h, c))),
                  pl.BlockSpec((1, dv), lambda b, h, c: (0, h))],
        out_specs=pl.BlockSpec((chunk, dv), lambda b, h, c: (row(b, h, c), h)),
        out_shape=jax.ShapeDtypeStruct((m, hv), BF16),
        scratch_shapes=[pltpu.VMEM((dk, dv + LANES), F32), pltpu.VMEM((1, 1), F32)],
        compiler_params=_params("parallel", "parallel", "arbitrary"),
        name="mlstm_core",
    )(qkv, qkv, qkv, o, gates, gates_t, gates_t, norm_g.reshape(1, hv).astype(F32))


def _fox_cum_kernel(g_ref, cum_ref, carry_ref, *, ts):
    @pl.when(pl.program_id(1) == 0)
    def _():
        carry_ref[...] = jnp.zeros_like(carry_ref)

    logf = _log_sigmoid(g_ref[...])
    ti = lax.broadcasted_iota(jnp.int32, (ts, ts), 0)
    si = lax.broadcasted_iota(jnp.int32, (ts, ts), 1)
    tri_lo = (si <= ti).astype(BF16)
    cum = sum(_dot(tri_lo, p) for p in _split3(logf)) + carry_ref[...]
    carry_ref[...] = cum[ts - 1:ts, :]
    cum_ref[...] = cum


def _fox_cumsum(gates, batch, seq):
    ts = _tile(seq, FOX_TS)
    ns = seq // ts
    return pl.pallas_call(
        functools.partial(_fox_cum_kernel, ts=ts),
        grid=(batch, ns),
        in_specs=[pl.BlockSpec((ts, LANES), lambda b, s: (b * ns + s, 0))],
        out_specs=pl.BlockSpec((ts, LANES), lambda b, s: (b * ns + s, 0)),
        out_shape=jax.ShapeDtypeStruct((batch * seq, LANES), F32),
        scratch_shapes=[pltpu.VMEM((1, LANES), F32)],
        compiler_params=_params("parallel", "arbitrary"),
        name="fox_cumsum",
    )(gates)


def _bias_lanes(col, own_first):
    n = col.shape[0]
    hi, mid, lo = (p.astype(F32) for p in _split3(col))
    lane = lax.broadcasted_iota(jnp.int32, (n, LANES), 1)
    base = 0 if own_first else 3
    ones = ((lane >= 3 - base) & (lane < 6 - base)).astype(F32)
    out = jnp.where(lane == base, hi, jnp.where(lane == base + 1, mid, jnp.where(lane == base + 2, lo, ones)))
    return out.astype(BF16)


def _fox_attn_kernel(q_ref, k_ref, v_ref, cc_ref, o_ref, ka_scr, vt_scr, m_scr, l_scr, acc_scr,
                     *, tq, seq):
    h = pl.program_id(1)
    i = pl.program_id(2)

    def head_col(rows):
        lane = lax.broadcasted_iota(jnp.int32, rows.shape, 1)
        return jnp.sum(jnp.where(lane == h, rows, 0.0), axis=1, keepdims=True) * LOG2E

    @pl.when(i == 0)
    def _():
        for r0 in range(0, seq, tq):
            ka_scr[r0:r0 + tq, :LANES] = k_ref[r0:r0 + tq, :]
            ka_scr[r0:r0 + tq, LANES:] = _bias_lanes(-head_col(cc_ref[r0:r0 + tq, :]), True)
            vt_scr[:, r0:r0 + tq] = v_ref[r0:r0 + tq, :].astype(F32).T.astype(BF16)

    q0 = pl.multiple_of(i * tq, tq)
    qa = jnp.concatenate([q_ref[...], _bias_lanes(head_col(cc_ref[pl.ds(q0, tq), :]), False)], axis=1)

    m_scr[...] = jnp.full_like(m_scr, NEG_BIG)
    l_scr[...] = jnp.zeros_like(l_scr)
    acc_scr[...] = jnp.zeros_like(acc_scr)

    def block(j, on_diagonal):
        start = pl.multiple_of(j * tq, tq)
        st = _dot_nt(ka_scr[pl.ds(start, tq), :], qa)
        if on_diagonal:
            ki = lax.broadcasted_iota(jnp.int32, (tq, tq), 0)
            qi = lax.broadcasted_iota(jnp.int32, (tq, tq), 1)
            st = jnp.where(ki <= qi, st, NEG_BIG)
        m_old = m_scr[...]
        m_new = jnp.maximum(m_old, jnp.max(st, axis=0, keepdims=True))
        p = jnp.exp2(st - m_new)
        a = jnp.exp2(m_old - m_new)
        l_scr[...] = a * l_scr[...] + jnp.sum(p, axis=0, keepdims=True)
        acc_scr[...] = a * acc_scr[...] + _dot(vt_scr[:, pl.ds(start, tq)], p.astype(BF16))
        m_scr[...] = m_new

    block(i, True)

    def body(j, carry):
        block(j, False)
        return carry

    lax.fori_loop(0, i, body, 0)
    o_ref[...] = (acc_scr[...] * (1.0 / l_scr[...])).T.astype(o_ref.dtype)


def _fox_attn(qkv, cum, batch, seq):
    dh = FOX_HEAD_DIM
    assert dh == LANES
    m = qkv.shape[0]
    heads = qkv.shape[1] // (3 * dh)
    assert heads <= LANES
    tq = _tile(seq, FOX_TQ)
    nq = seq // tq
    return pl.pallas_call(
        functools.partial(_fox_attn_kernel, tq=tq, seq=seq),
        grid=(batch, heads, nq),
        in_specs=[pl.BlockSpec((tq, dh), lambda b, h, i: (b * nq + i, h)),
                  pl.BlockSpec((seq, dh), lambda b, h, i: (b, heads + h)),
                  pl.BlockSpec((seq, dh), lambda b, h, i: (b, 2 * heads + h)),
                  pl.BlockSpec((seq, LANES), lambda b, h, i: (b, 0))],
        out_specs=pl.BlockSpec((tq, dh), lambda b, h, i: (b * nq + i, h)),
        out_shape=jax.ShapeDtypeStruct((m, heads * dh), BF16),
        scratch_shapes=[pltpu.VMEM((seq, 2 * LANES), BF16), pltpu.VMEM((dh, seq), BF16),
                        pltpu.VMEM((1, tq), F32), pltpu.VMEM((1, tq), F32),
                        pltpu.VMEM((dh, tq), F32)],
        compiler_params=_params("parallel", "parallel", "arbitrary"),
        name="fox_attn",
    )(qkv, qkv, qkv, cum)


def kernel(x, mlstm_w_in, mlstm_b_gate, mlstm_norm_g, mlstm_w_out, fox_w_in, fox_b_f, fox_w_out,
           ln_mix_g, ln_mix_b, ffn_w_up, ffn_w_down, ln_ffn_g, ln_ffn_b):
    batch, seq, d = x.shape
    depth = ln_mix_g.shape[0]
    alpha = (2 * depth) ** 0.25
    m = batch * seq
    ffn = ffn_w_down.shape[1]
    ffn_pad = -(-ffn // FFN_PAD) * FFN_PAD
    hv = mlstm_w_out.shape[1]
    mlstm_main = mlstm_w_in.shape[2] - 2 * MLSTM_HEADS
    mlstm_qk = (mlstm_main - 2 * hv) // 2
    mlstm_dk = mlstm_qk // MLSTM_HEADS
    fox_w = fox_w_out.shape[1]

    h32 = x.reshape(m, d)
    hbf = h32.astype(BF16)
    for layer in range(depth):
        slot = layer // N_MIXERS
        if layer % N_MIXERS == 0:
            qkv = _proj(hbf, mlstm_w_in, slot, 0, mlstm_main - hv, BF16, mlstm_qk, mlstm_dk ** -0.5)
            o = _proj(hbf, mlstm_w_in, slot, mlstm_main - hv, hv, F32)
            gates = _gate_proj(h32, mlstm_w_in[slot, :, mlstm_main:], mlstm_b_gate[slot])
            mixed = _mlstm_core(qkv, o, gates, gates.T, mlstm_norm_g[slot], batch, seq)
            w_out = _cast_pad_rows(mlstm_w_out, slot, hv)
        else:
            qkv = _proj(hbf, fox_w_in, slot, 0, 3 * fox_w, BF16, fox_w, FOX_HEAD_DIM ** -0.5 * LOG2E)
            gates = _gate_proj(h32, fox_w_in[slot, :, 3 * fox_w:], fox_b_f[slot])
            mixed = _fox_attn(qkv, _fox_cumsum(gates, batch, seq), batch, seq)
            w_out = _cast_pad_rows(fox_w_out, slot, fox_w)
        h32, hbf = _mm_res_ln(mixed, w_out, h32, ln_mix_g[layer], ln_mix_b[layer], alpha)

        hidden = _ffn_up(hbf, ffn_w_up, layer, ffn_pad)
        w_down = _cast_pad_rows(ffn_w_down, layer, ffn_pad)
        h32, hbf = _mm_res_ln(hidden, w_down, h32, ln_ffn_g[layer], ln_ffn_b[layer], alpha)
    return h32.reshape(batch, seq, d)
```

```python
import functools
import math

import jax
import jax.numpy as jnp
from jax import lax
from jax.experimental import pallas as pl
from jax.experimental.pallas import tpu as pltpu

F32 = jnp.float32
BF16 = jnp.bfloat16

MLSTM_HEADS = 8
FOX_HEAD_DIM = 128
GATE_SOFT_CAP = 15.0
HEAD_NORM_EPS = 1e-6
LN_EPS = 1e-5
N_MIXERS = 2

LANES = 128
SUBLANES = 8
NEG_BIG = -1e30
LOG2E = math.log2(math.e)

MM_TM = 1024
MM_TN = 512
FFN_TN = 256
LN_TM = 512
LN_TK = 512
LN_NCHUNK = 1024
LN_ROWS = 16
GATE_TM = 512
CAST_ROWS = 256
FFN_PAD = 1024
MLSTM_CHUNK = 256
FOX_TQ = 512
FOX_TS = 512
VMEM_LIMIT = 56 * 1024 * 1024


def _tile(n, pref):
    if n <= pref:
        return n
    t = (pref // LANES) * LANES
    while t >= LANES:
        if n % t == 0:
            return t
        t -= LANES
    raise ValueError(f"no lane-aligned tile of {n} below {pref}")


def _params(*sem):
    return pltpu.CompilerParams(dimension_semantics=sem, vmem_limit_bytes=VMEM_LIMIT)


def _dot(a, b):
    return jnp.dot(a, b, preferred_element_type=F32)


def _dot_nt(a, b):
    return lax.dot_general(a, b, (((1,), (1,)), ((), ())), preferred_element_type=F32)


def _dot_tn(a, b):
    return lax.dot_general(a, b, (((0,), (0,)), ((), ())), preferred_element_type=F32)


def _split3(x):
    hi = x.astype(BF16)
    r1 = x - hi.astype(F32)
    mid = r1.astype(BF16)
    lo = (r1 - mid.astype(F32)).astype(BF16)
    return hi, mid, lo


def _sigmoid(x):
    return 1.0 / (1.0 + jnp.exp(-x))


def _log_sigmoid(x):
    return jnp.minimum(x, 0.0) - jnp.log1p(jnp.exp(-jnp.abs(x)))


def _soft_cap(z):
    return GATE_SOFT_CAP * jnp.tanh(z / GATE_SOFT_CAP)


def _cast_kernel(w_ref, o_ref, *, n_real):
    r = pl.program_id(0)

    @pl.when(r < n_real)
    def _():
        o_ref[...] = w_ref[...].astype(BF16)

    @pl.when(r >= n_real)
    def _():
        o_ref[...] = jnp.zeros_like(o_ref)


def _cast_pad_rows(w_stack, idx, rows_pad):
    _, rows, cols = w_stack.shape
    tr = _tile(rows, CAST_ROWS)
    assert rows_pad % tr == 0
    n_real = rows // tr
    return pl.pallas_call(
        functools.partial(_cast_kernel, n_real=n_real),
        grid=(rows_pad // tr,),
        in_specs=[pl.BlockSpec((None, tr, cols), lambda r: (idx, jnp.minimum(r, n_real - 1), 0))],
        out_specs=pl.BlockSpec((tr, cols), lambda r: (r, 0)),
        out_shape=jax.ShapeDtypeStruct((rows_pad, cols), BF16),
        compiler_params=_params("parallel"),
        name="cast_w",
    )(w_stack)


def _proj_kernel(x_ref, w_ref, o_ref, wbf_ref, *, n_scaled, scale):
    j = pl.program_id(0)

    @pl.when(pl.program_id(1) == 0)
    def _():
        wbf_ref[...] = w_ref[...].astype(BF16)

    acc = _dot_nt(x_ref[...], wbf_ref[...])
    if n_scaled:
        acc = acc * jnp.where(j < n_scaled, scale, 1.0)
    o_ref[...] = acc.astype(o_ref.dtype)


def _proj(x, wt_stack, idx, col0, ncols, out_dtype, scaled_cols=0, scale=1.0):
    m, k = x.shape
    tm, tn = _tile(m, MM_TM), _tile(ncols, MM_TN)
    assert col0 % tn == 0 and scaled_cols % tn == 0
    j0 = col0 // tn
    return pl.pallas_call(
        functools.partial(_proj_kernel, n_scaled=scaled_cols // tn, scale=scale),
        grid=(ncols // tn, m // tm),
        in_specs=[pl.BlockSpec((tm, k), lambda j, i: (i, 0)),
                  pl.BlockSpec((None, tn, k), lambda j, i: (idx, j0 + j, 0))],
        out_specs=pl.BlockSpec((tm, tn), lambda j, i: (i, j)),
        out_shape=jax.ShapeDtypeStruct((m, ncols), out_dtype),
        scratch_shapes=[pltpu.VMEM((tn, k), BF16)],
        compiler_params=_params("arbitrary", "arbitrary"),
        name="proj",
    )(x, wt_stack)


def _gate_kernel(h_ref, w_ref, b_ref, o_ref, *, n_gates):
    x = h_ref[...]
    w = w_ref[...]
    x_hi = x.astype(BF16)
    x_lo = (x - x_hi.astype(F32)).astype(BF16)
    w_hi = w.astype(BF16)
    w_lo = (w - w_hi.astype(F32)).astype(BF16)
    acc = _dot_nt(x_hi, w_hi) + _dot_nt(x_lo, w_hi) + _dot_nt(x_hi, w_lo)
    lane = lax.broadcasted_iota(jnp.int32, acc.shape, 1)
    o_ref[...] = jnp.where(lane < n_gates, acc + b_ref[...], 0.0)


def _gate_proj(h32, wt_stack, idx, col0, bias):
    m, d = h32.shape
    g = wt_stack.shape[1] - col0
    assert col0 % LANES == 0 and 0 < g <= LANES
    b_pad = jnp.pad(bias.astype(F32), (0, LANES - g)).reshape(1, LANES)
    tm = _tile(m, GATE_TM)
    return pl.pallas_call(
        functools.partial(_gate_kernel, n_gates=g),
        grid=(m // tm,),
        in_specs=[pl.BlockSpec((tm, d), lambda i: (i, 0)),
                  pl.BlockSpec((None, LANES, d), lambda i: (idx, col0 // LANES, 0)),
                  pl.BlockSpec((1, LANES), lambda i: (0, 0))],
        out_specs=pl.BlockSpec((tm, LANES), lambda i: (i, 0)),
        out_shape=jax.ShapeDtypeStruct((m, LANES), F32),
        compiler_params=_params("parallel"),
        name="gate_proj",
    )(h32, wt_stack, b_pad)


def _mm_res_ln_kernel(a_ref, b_ref, h_ref, g_ref, beta_ref, o32_ref, obf_ref,
                      part_scr, mu_scr, rstd_scr, *, alpha, nk):
    k = pl.program_id(1)

    @pl.when(k == 0)
    def _():
        o32_ref[...] = alpha * h_ref[...]

    a = a_ref[...]
    tm, d = o32_ref.shape
    nch = min(d, LN_NCHUNK)
    for n0 in range(0, d, nch):
        o32_ref[:, n0:n0 + nch] += _dot(a, b_ref[:, n0:n0 + nch])

    @pl.when(k == nk - 1)
    def _():
        rb = min(tm, LN_ROWS)
        cols = [slice(c, c + LANES) for c in range(0, d, LANES)]

        def row_groups(fn):
            def body(r, carry):
                fn(pl.ds(pl.multiple_of(r * rb, rb), rb))
                return carry
            lax.fori_loop(0, tm // rb, body, 0)

        def lane_mean(part):
            total = jnp.sum(part, axis=1, keepdims=True) * (1.0 / d)
            return jnp.broadcast_to(total, part.shape)

        def tree_sum(xs):
            while len(xs) > 1:
                xs = [xs[n] + xs[n + 1] for n in range(0, len(xs) - 1, 2)] + xs[len(xs) & ~1:]
            return xs[0]

        def pass_sum(rows):
            part_scr[rows, :] = tree_sum([o32_ref[rows, c] for c in cols])

        def pass_sq(rows):
            mu = mu_scr[rows, :]
            dev = [o32_ref[rows, c] - mu for c in cols]
            part_scr[rows, :] = tree_sum([x * x for x in dev])

        def pass_norm(rows):
            mu = mu_scr[rows, :]
            rstd = rstd_scr[rows, :]
            for c in cols:
                out = (o32_ref[rows, c] - mu) * rstd * g_ref[:, c] + beta_ref[:, c]
                o32_ref[rows, c] = out
                obf_ref[rows, c] = out.astype(BF16)

        row_groups(pass_sum)
        mu_scr[...] = lane_mean(part_scr[...])
        row_groups(pass_sq)
        rstd_scr[...] = lax.rsqrt(lane_mean(part_scr[...]) + LN_EPS)
        row_groups(pass_norm)


def _mm_res_ln(a, b, h32, g, beta, alpha):
    m, kdim = a.shape
    d = b.shape[1]
    tm, tk = _tile(m, LN_TM), _tile(kdim, LN_TK)
    nk = kdim // tk
    return pl.pallas_call(
        functools.partial(_mm_res_ln_kernel, alpha=alpha, nk=nk),
        grid=(m // tm, nk),
        in_specs=[pl.BlockSpec((tm, tk), lambda i, k: (i, k)),
                  pl.BlockSpec((tk, d), lambda i, k: (k, 0)),
                  pl.BlockSpec((tm, d), lambda i, k: (i, 0)),
                  pl.BlockSpec((1, d), lambda i, k: (0, 0)),
                  pl.BlockSpec((1, d), lambda i, k: (0, 0))],
        out_specs=[pl.BlockSpec((tm, d), lambda i, k: (i, 0)),
                   pl.BlockSpec((tm, d), lambda i, k: (i, 0))],
        out_shape=[jax.ShapeDtypeStruct((m, d), F32), jax.ShapeDtypeStruct((m, d), BF16)],
        scratch_shapes=[pltpu.VMEM((tm, LANES), F32)] * 3,
        compiler_params=_params("parallel", "arbitrary"),
        name="proj_res_ln",
    )(a, b, h32, g.reshape(1, d).astype(F32), beta.reshape(1, d).astype(F32))


def _ffn_up_kernel(x_ref, wg_ref, wu_ref, o_ref, wg_bf, wu_bf, *, n_real):
    j = pl.program_id(0)

    @pl.when((pl.program_id(1) == 0) & (j < n_real))
    def _():
        wg_bf[...] = wg_ref[...].astype(BF16)
        wu_bf[...] = wu_ref[...].astype(BF16)

    @pl.when(j < n_real)
    def _():
        x = x_ref[...]
        gate = _dot(x, wg_bf[...])
        up = _dot(x, wu_bf[...])
        o_ref[...] = (gate * _sigmoid(gate) * up).astype(o_ref.dtype)

    @pl.when(j >= n_real)
    def _():
        o_ref[...] = jnp.zeros_like(o_ref)


def _ffn_up(x, w_up_stack, idx, ffn_pad):
    m, d = x.shape
    f = w_up_stack.shape[2] // 2
    tm, tn = _tile(m, MM_TM), _tile(f, FFN_TN)
    assert ffn_pad % tn == 0
    n_real = f // tn
    real = lambda j: jnp.minimum(j, n_real - 1)
    return pl.pallas_call(
        functools.partial(_ffn_up_kernel, n_real=n_real),
        grid=(ffn_pad // tn, m // tm),
        in_specs=[pl.BlockSpec((tm, d), lambda j, i: (i, 0)),
                  pl.BlockSpec((None, d, tn), lambda j, i: (idx, 0, real(j))),
                  pl.BlockSpec((None, d, tn), lambda j, i: (idx, 0, n_real + real(j)))],
        out_specs=pl.BlockSpec((tm, tn), lambda j, i: (i, j)),
        out_shape=jax.ShapeDtypeStruct((m, ffn_pad), BF16),
        scratch_shapes=[pltpu.VMEM((d, tn), BF16), pltpu.VMEM((d, tn), BF16)],
        compiler_params=_params("arbitrary", "arbitrary"),
        name="ffn_up",
    )(x, w_up_stack, w_up_stack)


def _mlstm_kernel(q_ref, k_ref, v_ref, o_ref, gc_ref, gri_ref, grf_ref, ng_ref, out_ref,
                  ct_ref, m_ref, *, chunk, dv, heads):
    h = pl.program_id(1)
    c = pl.program_id(2)
    L = chunk

    @pl.when(c == 0)
    def _():
        ct_ref[...] = jnp.zeros_like(ct_ref)
        m_ref[...] = jnp.zeros_like(m_ref)

    gc = gc_ref[...]
    lane = lax.broadcasted_iota(jnp.int32, (L, LANES), 1)
    ig_c = jnp.sum(jnp.where(lane == h, gc, 0.0), axis=1, keepdims=True)
    fg_c = jnp.sum(jnp.where(lane == heads + h, gc, 0.0), axis=1, keepdims=True)
    ig_r = gri_ref[pl.ds(h, 1), :]
    fg_r = grf_ref[pl.ds(h, 1), :]
    ig_c, ig_r = _soft_cap(ig_c), _soft_cap(ig_r)
    lf_c, lf_r = _log_sigmoid(_soft_cap(fg_c)), _log_sigmoid(_soft_cap(fg_r))

    ti = lax.broadcasted_iota(jnp.int32, (L, L), 0)
    si = lax.broadcasted_iota(jnp.int32, (L, L), 1)
    causal = si <= ti
    tri_lo = causal.astype(BF16)
    tri_up = (ti <= si).astype(BF16)
    b_c = sum(_dot(tri_lo, p) for p in _split3(jnp.broadcast_to(lf_c, (L, LANES))))[:, :1]
    b_r = sum(_dot(p, tri_up) for p in _split3(jnp.broadcast_to(lf_r, (SUBLANES, L))))[:1, :]

    m_prev = m_ref[...]
    d_log = jnp.where(causal, b_c - b_r + ig_r, NEG_BIG)
    g_log = b_c + m_prev
    m_t = jnp.maximum(g_log, jnp.max(d_log, axis=1, keepdims=True))
    w_intra = jnp.exp(d_log - m_t)
    w_inter = jnp.exp(g_log - m_t)

    q = q_ref[...]
    k = k_ref[...]
    v = v_ref[...]
    s = _dot_nt(q, k) * w_intra
    ct = ct_ref[...]
    inter = _dot(q, ct.astype(BF16))
    num = _dot(s.astype(BF16), v) + w_inter * inter[:, :dv]
    den = jnp.sum(s, axis=1, keepdims=True) + w_inter * inter[:, dv:dv + 1]
    hh = num * (1.0 / jnp.maximum(jnp.abs(den), jnp.exp(-m_t)))

    b_end = b_c[L - 1:L, :]
    a_c = b_end - b_c + ig_c
    a_r = b_end - b_r + ig_r
    m_new = jnp.maximum(b_end + m_prev, jnp.max(a_r, axis=1, keepdims=True))
    decay = jnp.exp(b_end + m_prev - m_new)
    wa_c = jnp.exp(a_c - m_new)
    one_lane = (lax.broadcasted_iota(jnp.int32, (L, LANES), 1) == 0).astype(F32)
    wv = jnp.concatenate([wa_c * v.astype(F32), wa_c * one_lane], axis=1).astype(BF16)
    ct_ref[...] = decay * ct + _dot_tn(k, wv)
    m_ref[...] = m_new

    hn = hh * lax.rsqrt(jnp.mean(hh * hh, axis=1, keepdims=True) + HEAD_NORM_EPS)
    out_ref[...] = (hn * ng_ref[...] * _sigmoid(o_ref[...])).astype(out_ref.dtype)


def _mlstm_core(qkv, o, gates, gates_t, norm_g, batch, seq):
    heads = MLSTM_HEADS
    assert heads == SUBLANES, "row-form gate blocks assume one sublane tile per gate kind"
    m = qkv.shape[0]
    hv = o.shape[1]
    dv = hv // heads
    dk = (qkv.shape[1] - hv) // (2 * heads)
    assert 2 * heads * dk == hv, "v blocks are addressed in units of dv"
    chunk = _tile(seq, MLSTM_CHUNK)
    nc = seq // chunk
    row = lambda b, h, c: b * nc + c
    return pl.pallas_call(
        functools.partial(_mlstm_kernel, chunk=chunk, dv=dv, heads=heads),
        grid=(batch, heads, nc),
        in_specs=[pl.BlockSpec((chunk, dk), lambda b, h, c: (row(b, h, c), h)),
                  pl.BlockSpec((chunk, dk), lambda b, h, c: (row(b, h, c), heads + h)),
                  pl.BlockSpec((chunk, dv), lambda b, h, c: (row(b, h, c), heads + h)),
                  pl.BlockSpec((chunk, dv), lambda b, h, c: (row(b, h, c), h)),
                  pl.BlockSpec((chunk, LANES), lambda b, h, c: (row(b, h, c), 0)),
                  pl.BlockSpec((SUBLANES, chunk), lambda b, h, c: (0, row(b, h, c))),
                  pl.BlockSpec((SUBLANES, chunk), lambda b, h, c: (1, row(b, h, c))),
                  pl.BlockSpec((1, dv), lambda b, h, c: (0, h))],
        out_specs=pl.BlockSpec((chunk, dv), lambda b, h, c: (row(b, h, c), h)),
        out_shape=jax.ShapeDtypeStruct((m, hv), BF16),
        scratch_shapes=[pltpu.VMEM((dk, dv + LANES), F32), pltpu.VMEM((1, 1), F32)],
        compiler_params=_params("parallel", "parallel", "arbitrary"),
        name="mlstm_core",
    )(qkv, qkv, qkv, o, gates, gates_t, gates_t, norm_g.reshape(1, hv).astype(F32))


def _fox_cum_kernel(g_ref, cum_ref, carry_ref, *, ts):
    @pl.when(pl.program_id(1) == 0)
    def _():
        carry_ref[...] = jnp.zeros_like(carry_ref)

    logf = _log_sigmoid(g_ref[...])
    ti = lax.broadcasted_iota(jnp.int32, (ts, ts), 0)
    si = lax.broadcasted_iota(jnp.int32, (ts, ts), 1)
    tri_lo = (si <= ti).astype(BF16)
    cum = sum(_dot(tri_lo, p) for p in _split3(logf)) + carry_ref[...]
    carry_ref[...] = cum[ts - 1:ts, :]
    cum_ref[...] = cum


def _fox_cumsum(gates, batch, seq):
    ts = _tile(seq, FOX_TS)
    ns = seq // ts
    return pl.pallas_call(
        functools.partial(_fox_cum_kernel, ts=ts),
        grid=(batch, ns),
        in_specs=[pl.BlockSpec((ts, LANES), lambda b, s: (b * ns + s, 0))],
        out_specs=pl.BlockSpec((ts, LANES), lambda b, s: (b * ns + s, 0)),
        out_shape=jax.ShapeDtypeStruct((batch * seq, LANES), F32),
        scratch_shapes=[pltpu.VMEM((1, LANES), F32)],
        compiler_params=_params("parallel", "arbitrary"),
        name="fox_cumsum",
    )(gates)


def _bias_lanes(col, own_first):
    n = col.shape[0]
    hi, mid, lo = (p.astype(F32) for p in _split3(col))
    lane = lax.broadcasted_iota(jnp.int32, (n, LANES), 1)
    base = 0 if own_first else 3
    ones = ((lane >= 3 - base) & (lane < 6 - base)).astype(F32)
    out = jnp.where(lane == base, hi, jnp.where(lane == base + 1, mid, jnp.where(lane == base + 2, lo, ones)))
    return out.astype(BF16)


def _fox_attn_kernel(q_ref, k_ref, v_ref, cc_ref, o_ref, ka_scr, vt_scr, s0_scr, s1_scr,
                     p0_scr, p1_scr, a_scr, m_scr, l_scr, acc_scr, *, tq, seq):
    h = pl.program_id(1)
    i = pl.program_id(2)

    def head_col(rows):
        lane = lax.broadcasted_iota(jnp.int32, rows.shape, 1)
        return jnp.sum(jnp.where(lane == h, rows, 0.0), axis=1, keepdims=True) * LOG2E

    @pl.when(i == 0)
    def _():
        for r0 in range(0, seq, tq):
            ka_scr[r0:r0 + tq, :LANES] = k_ref[r0:r0 + tq, :]
            ka_scr[r0:r0 + tq, LANES:] = _bias_lanes(-head_col(cc_ref[r0:r0 + tq, :]), True)
            vt_scr[:, r0:r0 + tq] = v_ref[r0:r0 + tq, :].astype(F32).T.astype(BF16)

    q0 = pl.multiple_of(i * tq, tq)
    qa = jnp.concatenate([q_ref[...], _bias_lanes(head_col(cc_ref[pl.ds(q0, tq), :]), False)], axis=1)

    s_bufs = (s0_scr, s1_scr)
    p_bufs = (p0_scr, p1_scr)
    m_scr[...] = jnp.full_like(m_scr, NEG_BIG)
    l_scr[...] = jnp.zeros_like(l_scr)
    acc_scr[...] = jnp.zeros_like(acc_scr)
    a_scr[...] = jnp.ones_like(a_scr)
    p1_scr[...] = jnp.zeros_like(p1_scr)

    def logits(j, parity):
        start = pl.multiple_of(j * tq, tq)
        s_bufs[parity][...] = _dot_nt(ka_scr[pl.ds(start, tq), :], qa)

    def softmax(parity, on_diagonal):
        st = s_bufs[parity][...]
        if on_diagonal:
            ki = lax.broadcasted_iota(jnp.int32, (tq, tq), 0)
            qi = lax.broadcasted_iota(jnp.int32, (tq, tq), 1)
            st = jnp.where(ki <= qi, st, NEG_BIG)
        m_old = m_scr[...]
        m_new = jnp.maximum(m_old, jnp.max(st, axis=0, keepdims=True))
        p = jnp.exp2(st - m_new)
        a = jnp.exp2(m_old - m_new)
        l_scr[...] = a * l_scr[...] + jnp.sum(p, axis=0, keepdims=True)
        m_scr[...] = m_new
        p_bufs[parity][...] = p.astype(BF16)
        a_scr[...] = a

    def values(j, parity):
        start = pl.multiple_of(jnp.maximum(j, 0) * tq, tq)
        acc_scr[...] = a_scr[...] * acc_scr[...] + _dot(vt_scr[:, pl.ds(start, tq)], p_bufs[parity][...])

    def step(j, parity):
        values(j - 1, 1 - parity)
        logits(j + 1, 1 - parity)
        softmax(parity, False)

    def finish(parity):
        values(i - 1, 1 - parity)
        softmax(parity, True)
        values(i, parity)

    logits(0, 0)

    def pair(t, carry):
        step(2 * t, 0)
        step(2 * t + 1, 1)
        return carry

    lax.fori_loop(0, i // 2, pair, 0)

    @pl.when(i % 2 == 0)
    def _():
        finish(0)

    @pl.when(i % 2 == 1)
    def _():
        step(i - 1, 0)
        finish(1)

    o_ref[...] = (acc_scr[...] * (1.0 / l_scr[...])).T.astype(o_ref.dtype)


def _fox_attn(qkv, cum, batch, seq):
    dh = FOX_HEAD_DIM
    assert dh == LANES
    m = qkv.shape[0]
    heads = qkv.shape[1] // (3 * dh)
    assert heads <= LANES
    tq = _tile(seq, FOX_TQ)
    nq = seq // tq
    return pl.pallas_call(
        functools.partial(_fox_attn_kernel, tq=tq, seq=seq),
        grid=(batch, heads, nq),
        in_specs=[pl.BlockSpec((tq, dh), lambda b, h, i: (b * nq + i, h)),
                  pl.BlockSpec((seq, dh), lambda b, h, i: (b, heads + h)),
                  pl.BlockSpec((seq, dh), lambda b, h, i: (b, 2 * heads + h)),
                  pl.BlockSpec((seq, LANES), lambda b, h, i: (b, 0))],
        out_specs=pl.BlockSpec((tq, dh), lambda b, h, i: (b * nq + i, h)),
        out_shape=jax.ShapeDtypeStruct((m, heads * dh), BF16),
        scratch_shapes=[pltpu.VMEM((seq, 2 * LANES), BF16), pltpu.VMEM((dh, seq), BF16),
                        pltpu.VMEM((tq, tq), F32), pltpu.VMEM((tq, tq), F32),
                        pltpu.VMEM((tq, tq), BF16), pltpu.VMEM((tq, tq), BF16),
                        pltpu.VMEM((1, tq), F32), pltpu.VMEM((1, tq), F32), pltpu.VMEM((1, tq), F32),
                        pltpu.VMEM((dh, tq), F32)],
        compiler_params=_params("parallel", "parallel", "arbitrary"),
        name="fox_attn",
    )(qkv, qkv, qkv, cum)


def kernel(x, mlstm_w_in, mlstm_b_gate, mlstm_norm_g, mlstm_w_out, fox_w_in, fox_b_f, fox_w_out,
           ln_mix_g, ln_mix_b, ffn_w_up, ffn_w_down, ln_ffn_g, ln_ffn_b):
    batch, seq, d = x.shape
    depth = ln_mix_g.shape[0]
    alpha = (2 * depth) ** 0.25
    m = batch * seq
    ffn = ffn_w_down.shape[1]
    ffn_pad = -(-ffn // FFN_PAD) * FFN_PAD
    hv = mlstm_w_out.shape[1]
    mlstm_main = mlstm_w_in.shape[2] - 2 * MLSTM_HEADS
    mlstm_qk = (mlstm_main - 2 * hv) // 2
    mlstm_dk = mlstm_qk // MLSTM_HEADS
    fox_w = fox_w_out.shape[1]

    mlstm_wt = jnp.swapaxes(mlstm_w_in, 1, 2)
    fox_wt = jnp.swapaxes(fox_w_in, 1, 2)

    h32 = x.reshape(m, d)
    hbf = h32.astype(BF16)
    for layer in range(depth):
        slot = layer // N_MIXERS
        if layer % N_MIXERS == 0:
            qkv = _proj(hbf, mlstm_wt, slot, 0, mlstm_main - hv, BF16, mlstm_qk, mlstm_dk ** -0.5)
            o = _proj(hbf, mlstm_wt, slot, mlstm_main - hv, hv, F32)
            gates = _gate_proj(h32, mlstm_wt, slot, mlstm_main, mlstm_b_gate[slot])
            mixed = _mlstm_core(qkv, o, gates, gates.T, mlstm_norm_g[slot], batch, seq)
            w_out = _cast_pad_rows(mlstm_w_out, slot, hv)
        else:
            qkv = _proj(hbf, fox_wt, slot, 0, 3 * fox_w, BF16, fox_w, FOX_HEAD_DIM ** -0.5 * LOG2E)
            gates = _gate_proj(h32, fox_wt, slot, 3 * fox_w, fox_b_f[slot])
            mixed = _fox_attn(qkv, _fox_cumsum(gates, batch, seq), batch, seq)
            w_out = _cast_pad_rows(fox_w_out, slot, fox_w)
        h32, hbf = _mm_res_ln(mixed, w_out, h32, ln_mix_g[layer], ln_mix_b[layer], alpha)

        hidden = _ffn_up(hbf, ffn_w_up, layer, ffn_pad)
        w_down = _cast_pad_rows(ffn_w_down, layer, ffn_pad)
        h32, hbf = _mm_res_ln(hidden, w_down, h32, ln_ffn_g[layer], ln_ffn_b[layer], alpha)
    return h32.reshape(batch, seq, d)
```

```python
import functools
import math

import jax
import jax.numpy as jnp
from jax import lax
from jax.experimental import pallas as pl
from jax.experimental.pallas import tpu as pltpu

F32 = jnp.float32
BF16 = jnp.bfloat16

MLSTM_HEADS = 8
FOX_HEAD_DIM = 128
GATE_SOFT_CAP = 15.0
HEAD_NORM_EPS = 1e-6
LN_EPS = 1e-5
N_MIXERS = 2

LANES = 128
SUBLANES = 8
NEG_BIG = -1e30
LOG2E = math.log2(math.e)

BIG_TM = 2048
MM_TN = 512
FFN_TN = 256
LN_TM = 512
LN_TK = 512
LN_NCHUNK = 1024
LN_ROWS = 16
GATE_TM = 512
CAST_ROWS = 256
FFN_PAD = 1024
MLSTM_CHUNK = 256
FOX_TQ = 512
FOX_TS = 512
VMEM_LIMIT = 56 * 1024 * 1024


def _tile(n, pref):
    if n <= pref:
        return n
    t = (pref // LANES) * LANES
    while t >= LANES:
        if n % t == 0:
            return t
        t -= LANES
    raise ValueError(f"no lane-aligned tile of {n} below {pref}")


def _params(*sem):
    return pltpu.CompilerParams(dimension_semantics=sem, vmem_limit_bytes=VMEM_LIMIT)


def _dot(a, b):
    return jnp.dot(a, b, preferred_element_type=F32)


def _dot_nt(a, b):
    return lax.dot_general(a, b, (((1,), (1,)), ((), ())), preferred_element_type=F32)


def _dot_tn(a, b):
    return lax.dot_general(a, b, (((0,), (0,)), ((), ())), preferred_element_type=F32)


def _split3(x):
    hi = x.astype(BF16)
    r1 = x - hi.astype(F32)
    mid = r1.astype(BF16)
    lo = (r1 - mid.astype(F32)).astype(BF16)
    return hi, mid, lo


def _sigmoid(x):
    return 1.0 / (1.0 + jnp.exp(-x))


def _log_sigmoid(x):
    return jnp.minimum(x, 0.0) - jnp.log1p(jnp.exp(-jnp.abs(x)))


def _soft_cap(z):
    return GATE_SOFT_CAP * jnp.tanh(z / GATE_SOFT_CAP)


def _cast_kernel(w_ref, o_ref, *, n_real):
    r = pl.program_id(0)

    @pl.when(r < n_real)
    def _():
        o_ref[...] = w_ref[...].astype(BF16)

    @pl.when(r >= n_real)
    def _():
        o_ref[...] = jnp.zeros_like(o_ref)


def _cast_pad_rows(w_stack, idx, rows_pad):
    _, rows, cols = w_stack.shape
    tr = _tile(rows, CAST_ROWS)
    assert rows_pad % tr == 0
    n_real = rows // tr
    return pl.pallas_call(
        functools.partial(_cast_kernel, n_real=n_real),
        grid=(rows_pad // tr,),
        in_specs=[pl.BlockSpec((None, tr, cols), lambda r: (idx, jnp.minimum(r, n_real - 1), 0))],
        out_specs=pl.BlockSpec((tr, cols), lambda r: (r, 0)),
        out_shape=jax.ShapeDtypeStruct((rows_pad, cols), BF16),
        compiler_params=_params("parallel"),
        name="cast_w",
    )(w_stack)


def _proj_kernel(x_ref, w_ref, o_ref, *, n_scaled, scale):
    j = pl.program_id(1)
    acc = _dot_nt(x_ref[...], w_ref[...].astype(BF16))
    if n_scaled:
        acc = acc * jnp.where(j < n_scaled, scale, 1.0)
    o_ref[...] = acc.astype(o_ref.dtype)


def _proj(x, wt_stack, idx, col0, ncols, out_dtype, scaled_cols=0, scale=1.0):
    m, k = x.shape
    tm, tn = _tile(m, BIG_TM), _tile(ncols, MM_TN)
    assert col0 % tn == 0 and scaled_cols % tn == 0
    j0 = col0 // tn
    return pl.pallas_call(
        functools.partial(_proj_kernel, n_scaled=scaled_cols // tn, scale=scale),
        grid=(m // tm, ncols // tn),
        in_specs=[pl.BlockSpec((tm, k), lambda i, j: (i, 0), pipeline_mode=pl.Buffered(1)),
                  pl.BlockSpec((None, tn, k), lambda i, j: (idx, j0 + j, 0))],
        out_specs=pl.BlockSpec((tm, tn), lambda i, j: (i, j)),
        out_shape=jax.ShapeDtypeStruct((m, ncols), out_dtype),
        compiler_params=_params("parallel", "arbitrary"),
        name="proj",
    )(x, wt_stack)


def _gate_kernel(h_ref, w_ref, b_ref, o_ref, *, n_gates):
    x = h_ref[...]
    w = w_ref[...]
    x_hi = x.astype(BF16)
    x_lo = (x - x_hi.astype(F32)).astype(BF16)
    w_hi = w.astype(BF16)
    w_lo = (w - w_hi.astype(F32)).astype(BF16)
    acc = _dot_nt(x_hi, w_hi) + _dot_nt(x_lo, w_hi) + _dot_nt(x_hi, w_lo)
    lane = lax.broadcasted_iota(jnp.int32, acc.shape, 1)
    o_ref[...] = jnp.where(lane < n_gates, acc + b_ref[...], 0.0)


def _gate_proj(h32, wt_stack, idx, col0, bias):
    m, d = h32.shape
    g = wt_stack.shape[1] - col0
    assert col0 % LANES == 0 and 0 < g <= LANES
    b_pad = jnp.pad(bias.astype(F32), (0, LANES - g)).reshape(1, LANES)
    tm = _tile(m, GATE_TM)
    return pl.pallas_call(
        functools.partial(_gate_kernel, n_gates=g),
        grid=(m // tm,),
        in_specs=[pl.BlockSpec((tm, d), lambda i: (i, 0)),
                  pl.BlockSpec((None, LANES, d), lambda i: (idx, col0 // LANES, 0)),
                  pl.BlockSpec((1, LANES), lambda i: (0, 0))],
        out_specs=pl.BlockSpec((tm, LANES), lambda i: (i, 0)),
        out_shape=jax.ShapeDtypeStruct((m, LANES), F32),
        compiler_params=_params("parallel"),
        name="gate_proj",
    )(h32, wt_stack, b_pad)


def _mm_res_ln_kernel(a_ref, b_ref, h_ref, g_ref, beta_ref, o32_ref, obf_ref,
                      part_scr, mu_scr, rstd_scr, *, alpha, nk):
    k = pl.program_id(1)

    @pl.when(k == 0)
    def _():
        o32_ref[...] = alpha * h_ref[...]

    a = a_ref[...]
    tm, d = o32_ref.shape
    nch = min(d, LN_NCHUNK)
    for n0 in range(0, d, nch):
        o32_ref[:, n0:n0 + nch] += _dot(a, b_ref[:, n0:n0 + nch])

    @pl.when(k == nk - 1)
    def _():
        rb = min(tm, LN_ROWS)
        cols = [slice(c, c + LANES) for c in range(0, d, LANES)]

        def row_groups(fn):
            def body(r, carry):
                fn(pl.ds(pl.multiple_of(r * rb, rb), rb))
                return carry
            lax.fori_loop(0, tm // rb, body, 0)

        def lane_mean(part):
            total = jnp.sum(part, axis=1, keepdims=True) * (1.0 / d)
            return jnp.broadcast_to(total, part.shape)

        def tree_sum(xs):
            while len(xs) > 1:
                xs = [xs[n] + xs[n + 1] for n in range(0, len(xs) - 1, 2)] + xs[len(xs) & ~1:]
            return xs[0]

        def pass_sum(rows):
            part_scr[rows, :] = tree_sum([o32_ref[rows, c] for c in cols])

        def pass_sq(rows):
            mu = mu_scr[rows, :]
            dev = [o32_ref[rows, c] - mu for c in cols]
            part_scr[rows, :] = tree_sum([x * x for x in dev])

        def pass_norm(rows):
            mu = mu_scr[rows, :]
            rstd = rstd_scr[rows, :]
            for c in cols:
                out = (o32_ref[rows, c] - mu) * rstd * g_ref[:, c] + beta_ref[:, c]
                o32_ref[rows, c] = out
                obf_ref[rows, c] = out.astype(BF16)

        row_groups(pass_sum)
        mu_scr[...] = lane_mean(part_scr[...])
        row_groups(pass_sq)
        rstd_scr[...] = lax.rsqrt(lane_mean(part_scr[...]) + LN_EPS)
        row_groups(pass_norm)


def _mm_res_ln(a, b, h32, g, beta, alpha):
    m, kdim = a.shape
    d = b.shape[1]
    tm, tk = _tile(m, LN_TM), _tile(kdim, LN_TK)
    nk = kdim // tk
    return pl.pallas_call(
        functools.partial(_mm_res_ln_kernel, alpha=alpha, nk=nk),
        grid=(m // tm, nk),
        in_specs=[pl.BlockSpec((tm, tk), lambda i, k: (i, k)),
                  pl.BlockSpec((tk, d), lambda i, k: (k, 0)),
                  pl.BlockSpec((tm, d), lambda i, k: (i, 0)),
                  pl.BlockSpec((1, d), lambda i, k: (0, 0)),
                  pl.BlockSpec((1, d), lambda i, k: (0, 0))],
        out_specs=[pl.BlockSpec((tm, d), lambda i, k: (i, 0)),
                   pl.BlockSpec((tm, d), lambda i, k: (i, 0))],
        out_shape=[jax.ShapeDtypeStruct((m, d), F32), jax.ShapeDtypeStruct((m, d), BF16)],
        scratch_shapes=[pltpu.VMEM((tm, LANES), F32)] * 3,
        compiler_params=_params("parallel", "arbitrary"),
        name="proj_res_ln",
    )(a, b, h32, g.reshape(1, d).astype(F32), beta.reshape(1, d).astype(F32))


def _ffn_up_kernel(x_ref, wg_ref, wu_ref, o_ref, *, n_real):
    j = pl.program_id(1)

    @pl.when(j < n_real)
    def _():
        x = x_ref[...]
        gate = _dot(x, wg_ref[...].astype(BF16))
        up = _dot(x, wu_ref[...].astype(BF16))
        o_ref[...] = (gate * _sigmoid(gate) * up).astype(o_ref.dtype)

    @pl.when(j >= n_real)
    def _():
        o_ref[...] = jnp.zeros_like(o_ref)


def _ffn_up(x, w_up_stack, idx, ffn_pad):
    m, d = x.shape
    f = w_up_stack.shape[2] // 2
    tm, tn = _tile(m, BIG_TM), _tile(f, FFN_TN)
    assert ffn_pad % tn == 0
    n_real = f // tn
    real = lambda j: jnp.minimum(j, n_real - 1)
    return pl.pallas_call(
        functools.partial(_ffn_up_kernel, n_real=n_real),
        grid=(m // tm, ffn_pad // tn),
        in_specs=[pl.BlockSpec((tm, d), lambda i, j: (i, 0), pipeline_mode=pl.Buffered(1)),
                  pl.BlockSpec((None, d, tn), lambda i, j: (idx, 0, real(j))),
                  pl.BlockSpec((None, d, tn), lambda i, j: (idx, 0, n_real + real(j)))],
        out_specs=pl.BlockSpec((tm, tn), lambda i, j: (i, j)),
        out_shape=jax.ShapeDtypeStruct((m, ffn_pad), BF16),
        compiler_params=_params("parallel", "arbitrary"),
        name="ffn_up",
    )(x, w_up_stack, w_up_stack)


def _mlstm_kernel(q_ref, k_ref, v_ref, o_ref, gc_ref, gri_ref, grf_ref, ng_ref, out_ref,
                  ct_ref, m_ref, *, chunk, dv, heads):
    h = pl.program_id(1)
    c = pl.program_id(2)
    L = chunk

    @pl.when(c == 0)
    def _():
        ct_ref[...] = jnp.zeros_like(ct_ref)
        m_ref[...] = jnp.zeros_like(m_ref)

    gc = gc_ref[...]
    lane = lax.broadcasted_iota(jnp.int32, (L, LANES), 1)
    ig_c = jnp.sum(jnp.where(lane == h, gc, 0.0), axis=1, keepdims=True)
    fg_c = jnp.sum(jnp.where(lane == heads + h, gc, 0.0), axis=1, keepdims=True)
    ig_r = gri_ref[pl.ds(h, 1), :]
    fg_r = grf_ref[pl.ds(h, 1), :]
    ig_c, ig_r = _soft_cap(ig_c), _soft_cap(ig_r)
    lf_c, lf_r = _log_sigmoid(_soft_cap(fg_c)), _log_sigmoid(_soft_cap(fg_r))

    ti = lax.broadcasted_iota(jnp.int32, (L, L), 0)
    si = lax.broadcasted_iota(jnp.int32, (L, L), 1)
    causal = si <= ti
    tri_lo = causal.astype(BF16)
    tri_up = (ti <= si).astype(BF16)
    b_c = sum(_dot(tri_lo, p) for p in _split3(jnp.broadcast_to(lf_c, (L, LANES))))[:, :1]
    b_r = sum(_dot(p, tri_up) for p in _split3(jnp.broadcast_to(lf_r, (SUBLANES, L))))[:1, :]

    m_prev = m_ref[...]
    d_log = jnp.where(causal, b_c - b_r + ig_r, NEG_BIG)
    g_log = b_c + m_prev
    m_t = jnp.maximum(g_log, jnp.max(d_log, axis=1, keepdims=True))
    w_intra = jnp.exp(d_log - m_t)
    w_inter = jnp.exp(g_log - m_t)

    q = q_ref[...]
    k = k_ref[...]
    v = v_ref[...]
    s = _dot_nt(q, k) * w_intra
    ct = ct_ref[...]
    inter = _dot(q, ct.astype(BF16))
    num = _dot(s.astype(BF16), v) + w_inter * inter[:, :dv]
    den = jnp.sum(s, axis=1, keepdims=True) + w_inter * inter[:, dv:dv + 1]
    hh = num * (1.0 / jnp.maximum(jnp.abs(den), jnp.exp(-m_t)))

    b_end = b_c[L - 1:L, :]
    a_c = b_end - b_c + ig_c
    a_r = b_end - b_r + ig_r
    m_new = jnp.maximum(b_end + m_prev, jnp.max(a_r, axis=1, keepdims=True))
    decay = jnp.exp(b_end + m_prev - m_new)
    wa_c = jnp.exp(a_c - m_new)
    one_lane = (lax.broadcasted_iota(jnp.int32, (L, LANES), 1) == 0).astype(F32)
    wv = jnp.concatenate([wa_c * v.astype(F32), wa_c * one_lane], axis=1).astype(BF16)
    ct_ref[...] = decay * ct + _dot_tn(k, wv)
    m_ref[...] = m_new

    hn = hh * lax.rsqrt(jnp.mean(hh * hh, axis=1, keepdims=True) + HEAD_NORM_EPS)
    out_ref[...] = (hn * ng_ref[...] * _sigmoid(o_ref[...])).astype(out_ref.dtype)


def _mlstm_core(qkv, o, gates, gates_t, norm_g, batch, seq):
    heads = MLSTM_HEADS
    assert heads == SUBLANES, "row-form gate blocks assume one sublane tile per gate kind"
    m = qkv.shape[0]
    hv = o.shape[1]
    dv = hv // heads
    dk = (qkv.shape[1] - hv) // (2 * heads)
    assert 2 * heads * dk == hv, "v blocks are addressed in units of dv"
    chunk = _tile(seq, MLSTM_CHUNK)
    nc = seq // chunk
    row = lambda b, h, c: b * nc + c
    return pl.pallas_call(
        functools.partial(_mlstm_kernel, chunk=chunk, dv=dv, heads=heads),
        grid=(batch, heads, nc),
        in_specs=[pl.BlockSpec((chunk, dk), lambda b, h, c: (row(b, h, c), h)),
                  pl.BlockSpec((chunk, dk), lambda b, h, c: (row(b, h, c), heads + h)),
                  pl.BlockSpec((chunk, dv), lambda b, h, c: (row(b, h, c), heads + h)),
                  pl.BlockSpec((chunk, dv), lambda b, h, c: (row(b, h, c), h)),
                  pl.BlockSpec((chunk, LANES), lambda b, h, c: (row(b, h, c), 0)),
                  pl.BlockSpec((SUBLANES, chunk), lambda b, h, c: (0, row(b, h, c))),
                  pl.BlockSpec((SUBLANES, chunk), lambda b, h, c: (1, row(b, h, c))),
                  pl.BlockSpec((1, dv), lambda b, h, c: (0, h))],
        out_specs=pl.BlockSpec((chunk, dv), lambda b, h, c: (row(b, h, c), h)),
        out_shape=jax.ShapeDtypeStruct((m, hv), BF16),
        scratch_shapes=[pltpu.VMEM((dk, dv + LANES), F32), pltpu.VMEM((1, 1), F32)],
        compiler_params=_params("parallel", "parallel", "arbitrary"),
        name="mlstm_core",
    )(qkv, qkv, qkv, o, gates, gates_t, gates_t, norm_g.reshape(1, hv).astype(F32))


def _fox_cum_kernel(g_ref, cum_ref, carry_ref, *, ts):
    @pl.when(pl.program_id(1) == 0)
    def _():
        carry_ref[...] = jnp.zeros_like(carry_ref)

    logf = _log_sigmoid(g_ref[...])
    ti = lax.broadcasted_iota(jnp.int32, (ts, ts), 0)
    si = lax.broadcasted_iota(jnp.int32, (ts, ts), 1)
    tri_lo = (si <= ti).astype(BF16)
    cum = sum(_dot(tri_lo, p) for p in _split3(logf)) + carry_ref[...]
    carry_ref[...] = cum[ts - 1:ts, :]
    cum_ref[...] = cum


def _fox_cumsum(gates, batch, seq):
    ts = _tile(seq, FOX_TS)
    ns = seq // ts
    return pl.pallas_call(
        functools.partial(_fox_cum_kernel, ts=ts),
        grid=(batch, ns),
        in_specs=[pl.BlockSpec((ts, LANES), lambda b, s: (b * ns + s, 0))],
        out_specs=pl.BlockSpec((ts, LANES), lambda b, s: (b * ns + s, 0)),
        out_shape=jax.ShapeDtypeStruct((batch * seq, LANES), F32),
        scratch_shapes=[pltpu.VMEM((1, LANES), F32)],
        compiler_params=_params("parallel", "arbitrary"),
        name="fox_cumsum",
    )(gates)


def _bias_lanes(col, own_first):
    n = col.shape[0]
    hi, mid, lo = (p.astype(F32) for p in _split3(col))
    lane = lax.broadcasted_iota(jnp.int32, (n, LANES), 1)
    base = 0 if own_first else 3
    ones = ((lane >= 3 - base) & (lane < 6 - base)).astype(F32)
    out = jnp.where(lane == base, hi, jnp.where(lane == base + 1, mid, jnp.where(lane == base + 2, lo, ones)))
    return out.astype(BF16)


def _fox_attn_kernel(q_ref, k_ref, v_ref, cc_ref, o_ref, ka_scr, vt_scr, s0_scr, s1_scr,
                     p0_scr, p1_scr, a_scr, m_scr, l_scr, acc_scr, *, tq, seq):
    h = pl.program_id(1)
    i = pl.program_id(2)

    def head_col(rows):
        lane = lax.broadcasted_iota(jnp.int32, rows.shape, 1)
        return jnp.sum(jnp.where(lane == h, rows, 0.0), axis=1, keepdims=True) * LOG2E

    @pl.when(i == 0)
    def _():
        for r0 in range(0, seq, tq):
            ka_scr[r0:r0 + tq, :LANES] = k_ref[r0:r0 + tq, :]
            ka_scr[r0:r0 + tq, LANES:] = _bias_lanes(-head_col(cc_ref[r0:r0 + tq, :]), True)
            vt_scr[:, r0:r0 + tq] = v_ref[r0:r0 + tq, :].astype(F32).T.astype(BF16)

    q0 = pl.multiple_of(i * tq, tq)
    qa = jnp.concatenate([q_ref[...], _bias_lanes(head_col(cc_ref[pl.ds(q0, tq), :]), False)], axis=1)

    s_bufs = (s0_scr, s1_scr)
    p_bufs = (p0_scr, p1_scr)
    m_scr[...] = jnp.full_like(m_scr, NEG_BIG)
    l_scr[...] = jnp.zeros_like(l_scr)
    acc_scr[...] = jnp.zeros_like(acc_scr)
    a_scr[...] = jnp.ones_like(a_scr)
    p1_scr[...] = jnp.zeros_like(p1_scr)

    def logits(j, parity):
        start = pl.multiple_of(j * tq, tq)
        s_bufs[parity][...] = _dot_nt(ka_scr[pl.ds(start, tq), :], qa)

    def softmax(parity, on_diagonal):
        st = s_bufs[parity][...]
        if on_diagonal:
            ki = lax.broadcasted_iota(jnp.int32, (tq, tq), 0)
            qi = lax.broadcasted_iota(jnp.int32, (tq, tq), 1)
            st = jnp.where(ki <= qi, st, NEG_BIG)
        m_old = m_scr[...]
        m_new = jnp.maximum(m_old, jnp.max(st, axis=0, keepdims=True))
        p = jnp.exp2(st - m_new)
        a = jnp.exp2(m_old - m_new)
        l_scr[...] = a * l_scr[...] + jnp.sum(p, axis=0, keepdims=True)
        m_scr[...] = m_new
        p_bufs[parity][...] = p.astype(BF16)
        a_scr[...] = a

    def values(j, parity):
        start = pl.multiple_of(jnp.maximum(j, 0) * tq, tq)
        acc_scr[...] = a_scr[...] * acc_scr[...] + _dot(vt_scr[:, pl.ds(start, tq)], p_bufs[parity][...])

    def step(j, parity):
        values(j - 1, 1 - parity)
        logits(j + 1, 1 - parity)
        softmax(parity, False)

    def finish(parity):
        values(i - 1, 1 - parity)
        softmax(parity, True)
        values(i, parity)

    logits(0, 0)

    def pair(t, carry):
        step(2 * t, 0)
        step(2 * t + 1, 1)
        return carry

    lax.fori_loop(0, i // 2, pair, 0)

    @pl.when(i % 2 == 0)
    def _():
        finish(0)

    @pl.when(i % 2 == 1)
    def _():
        step(i - 1, 0)
        finish(1)

    o_ref[...] = (acc_scr[...] * (1.0 / l_scr[...])).T.astype(o_ref.dtype)


def _fox_attn(qkv, cum, batch, seq):
    dh = FOX_HEAD_DIM
    assert dh == LANES
    m = qkv.shape[0]
    heads = qkv.shape[1] // (3 * dh)
    assert heads <= LANES
    tq = _tile(seq, FOX_TQ)
    nq = seq // tq
    return pl.pallas_call(
        functools.partial(_fox_attn_kernel, tq=tq, seq=seq),
        grid=(batch, heads, nq),
        in_specs=[pl.BlockSpec((tq, dh), lambda b, h, i: (b * nq + i, h)),
                  pl.BlockSpec((seq, dh), lambda b, h, i: (b, heads + h)),
                  pl.BlockSpec((seq, dh), lambda b, h, i: (b, 2 * heads + h)),
                  pl.BlockSpec((seq, LANES), lambda b, h, i: (b, 0))],
        out_specs=pl.BlockSpec((tq, dh), lambda b, h, i: (b * nq + i, h)),
        out_shape=jax.ShapeDtypeStruct((m, heads * dh), BF16),
        scratch_shapes=[pltpu.VMEM((seq, 2 * LANES), BF16), pltpu.VMEM((dh, seq), BF16),
                        pltpu.VMEM((tq, tq), F32), pltpu.VMEM((tq, tq), F32),
                        pltpu.VMEM((tq, tq), BF16), pltpu.VMEM((tq, tq), BF16),
                        pltpu.VMEM((1, tq), F32), pltpu.VMEM((1, tq), F32), pltpu.VMEM((1, tq), F32),
                        pltpu.VMEM((dh, tq), F32)],
        compiler_params=_params("parallel", "parallel", "arbitrary"),
        name="fox_attn",
    )(qkv, qkv, qkv, cum)


def kernel(x, mlstm_w_in, mlstm_b_gate, mlstm_norm_g, mlstm_w_out, fox_w_in, fox_b_f, fox_w_out,
           ln_mix_g, ln_mix_b, ffn_w_up, ffn_w_down, ln_ffn_g, ln_ffn_b):
    batch, seq, d = x.shape
    depth = ln_mix_g.shape[0]
    alpha = (2 * depth) ** 0.25
    m = batch * seq
    ffn = ffn_w_down.shape[1]
    ffn_pad = -(-ffn // FFN_PAD) * FFN_PAD
    hv = mlstm_w_out.shape[1]
    mlstm_main = mlstm_w_in.shape[2] - 2 * MLSTM_HEADS
    mlstm_qk = (mlstm_main - 2 * hv) // 2
    mlstm_dk = mlstm_qk // MLSTM_HEADS
    fox_w = fox_w_out.shape[1]

    mlstm_wt = jnp.swapaxes(mlstm_w_in, 1, 2)
    fox_wt = jnp.swapaxes(fox_w_in, 1, 2)

    h32 = x.reshape(m, d)
    hbf = h32.astype(BF16)
    for layer in range(depth):
        slot = layer // N_MIXERS
        if layer % N_MIXERS == 0:
            qkv = _proj(hbf, mlstm_wt, slot, 0, mlstm_main - hv, BF16, mlstm_qk, mlstm_dk ** -0.5)
            o = _proj(hbf, mlstm_wt, slot, mlstm_main - hv, hv, F32)
            gates = _gate_proj(h32, mlstm_wt, slot, mlstm_main, mlstm_b_gate[slot])
            mixed = _mlstm_core(qkv, o, gates, gates.T, mlstm_norm_g[slot], batch, seq)
            w_out = _cast_pad_rows(mlstm_w_out, slot, hv)
        else:
            qkv = _proj(hbf, fox_wt, slot, 0, 3 * fox_w, BF16, fox_w, FOX_HEAD_DIM ** -0.5 * LOG2E)
            gates = _gate_proj(h32, fox_wt, slot, 3 * fox_w, fox_b_f[slot])
            mixed = _fox_attn(qkv, _fox_cumsum(gates, batch, seq), batch, seq)
            w_out = _cast_pad_rows(fox_w_out, slot, fox_w)
        h32, hbf = _mm_res_ln(mixed, w_out, h32, ln_mix_g[layer], ln_mix_b[layer], alpha)

        hidden = _ffn_up(hbf, ffn_w_up, layer, ffn_pad)
        w_down = _cast_pad_rows(ffn_w_down, layer, ffn_pad)
        h32, hbf = _mm_res_ln(hidden, w_down, h32, ln_ffn_g[layer], ln_ffn_b[layer], alpha)
    return h32.reshape(batch, seq, d)
```

```python
import functools
import math

import jax
import jax.numpy as jnp
from jax import lax
from jax.experimental import pallas as pl
from jax.experimental.pallas import tpu as pltpu

F32 = jnp.float32
BF16 = jnp.bfloat16

MLSTM_HEADS = 8
FOX_HEAD_DIM = 128
GATE_SOFT_CAP = 15.0
HEAD_NORM_EPS = 1e-6
LN_EPS = 1e-5
N_MIXERS = 2

LANES = 128
SUBLANES = 8
NEG_BIG = -1e30
LOG2E = math.log2(math.e)

BIG_TM = 2048
MM_TN = 512
FFN_TN = 256
LN_TM = 512
LN_TK = 512
LN_NCHUNK = 1024
LN_ROWS = 16
GATE_TM = 512
CAST_ROWS = 256
FFN_PAD = 1024
MLSTM_CHUNK = 256
MLSTM_HEADS_PER_STEP = 4
FOX_TQ = 512
FOX_TS = 512
FOX_HEADS_PER_STEP = 4
VMEM_LIMIT = 56 * 1024 * 1024


def _tile(n, pref):
    if n <= pref:
        return n
    t = (pref // LANES) * LANES
    while t >= LANES:
        if n % t == 0:
            return t
        t -= LANES
    raise ValueError(f"no lane-aligned tile of {n} below {pref}")


def _params(*sem):
    return pltpu.CompilerParams(dimension_semantics=sem, vmem_limit_bytes=VMEM_LIMIT)


def _dot(a, b):
    return jnp.dot(a, b, preferred_element_type=F32)


def _dot_nt(a, b):
    return lax.dot_general(a, b, (((1,), (1,)), ((), ())), preferred_element_type=F32)


def _dot_tn(a, b):
    return lax.dot_general(a, b, (((0,), (0,)), ((), ())), preferred_element_type=F32)


def _split3(x):
    hi = x.astype(BF16)
    r1 = x - hi.astype(F32)
    mid = r1.astype(BF16)
    lo = (r1 - mid.astype(F32)).astype(BF16)
    return hi, mid, lo


def _sigmoid(x):
    return 1.0 / (1.0 + jnp.exp(-x))


def _log_sigmoid(x):
    return jnp.minimum(x, 0.0) - jnp.log1p(jnp.exp(-jnp.abs(x)))


def _soft_cap(z):
    return GATE_SOFT_CAP * jnp.tanh(z / GATE_SOFT_CAP)


def _cast_kernel(w_ref, o_ref, *, n_real):
    r = pl.program_id(0)

    @pl.when(r < n_real)
    def _():
        o_ref[...] = w_ref[...].astype(BF16)

    @pl.when(r >= n_real)
    def _():
        o_ref[...] = jnp.zeros_like(o_ref)


def _cast_pad_rows(w_stack, idx, rows_pad):
    _, rows, cols = w_stack.shape
    tr = _tile(rows, CAST_ROWS)
    assert rows_pad % tr == 0
    n_real = rows // tr
    return pl.pallas_call(
        functools.partial(_cast_kernel, n_real=n_real),
        grid=(rows_pad // tr,),
        in_specs=[pl.BlockSpec((None, tr, cols), lambda r: (idx, jnp.minimum(r, n_real - 1), 0))],
        out_specs=pl.BlockSpec((tr, cols), lambda r: (r, 0)),
        out_shape=jax.ShapeDtypeStruct((rows_pad, cols), BF16),
        compiler_params=_params("parallel"),
        name="cast_w",
    )(w_stack)


def _proj_kernel(x_ref, w_ref, o_ref, *, n_scaled, scale):
    j = pl.program_id(1)
    acc = _dot_nt(x_ref[...], w_ref[...].astype(BF16))
    if n_scaled:
        acc = acc * jnp.where(j < n_scaled, scale, 1.0)
    o_ref[...] = acc.astype(o_ref.dtype)


def _proj(x, wt_stack, idx, col0, ncols, out_dtype, scaled_cols=0, scale=1.0):
    m, k = x.shape
    tm, tn = _tile(m, BIG_TM), _tile(ncols, MM_TN)
    assert col0 % tn == 0 and scaled_cols % tn == 0
    j0 = col0 // tn
    return pl.pallas_call(
        functools.partial(_proj_kernel, n_scaled=scaled_cols // tn, scale=scale),
        grid=(m // tm, ncols // tn),
        in_specs=[pl.BlockSpec((tm, k), lambda i, j: (i, 0), pipeline_mode=pl.Buffered(1)),
                  pl.BlockSpec((None, tn, k), lambda i, j: (idx, j0 + j, 0))],
        out_specs=pl.BlockSpec((tm, tn), lambda i, j: (i, j)),
        out_shape=jax.ShapeDtypeStruct((m, ncols), out_dtype),
        compiler_params=_params("parallel", "arbitrary"),
        name="proj",
    )(x, wt_stack)


def _gate_kernel(h_ref, w_ref, b_ref, o_ref, *, n_gates):
    x = h_ref[...]
    w = w_ref[...]
    x_hi = x.astype(BF16)
    x_lo = (x - x_hi.astype(F32)).astype(BF16)
    w_hi = w.astype(BF16)
    w_lo = (w - w_hi.astype(F32)).astype(BF16)
    acc = _dot_nt(x_hi, w_hi) + _dot_nt(x_lo, w_hi) + _dot_nt(x_hi, w_lo)
    lane = lax.broadcasted_iota(jnp.int32, acc.shape, 1)
    o_ref[...] = jnp.where(lane < n_gates, acc + b_ref[...], 0.0)


def _gate_proj(h32, wt_stack, idx, col0, bias):
    m, d = h32.shape
    g = wt_stack.shape[1] - col0
    assert col0 % LANES == 0 and 0 < g <= LANES
    b_pad = jnp.pad(bias.astype(F32), (0, LANES - g)).reshape(1, LANES)
    tm = _tile(m, GATE_TM)
    return pl.pallas_call(
        functools.partial(_gate_kernel, n_gates=g),
        grid=(m // tm,),
        in_specs=[pl.BlockSpec((tm, d), lambda i: (i, 0)),
                  pl.BlockSpec((None, LANES, d), lambda i: (idx, col0 // LANES, 0)),
                  pl.BlockSpec((1, LANES), lambda i: (0, 0))],
        out_specs=pl.BlockSpec((tm, LANES), lambda i: (i, 0)),
        out_shape=jax.ShapeDtypeStruct((m, LANES), F32),
        compiler_params=_params("parallel"),
        name="gate_proj",
    )(h32, wt_stack, b_pad)


def _mm_res_ln_kernel(a_ref, b_ref, h_ref, g_ref, beta_ref, o32_ref, obf_ref,
                      part_scr, mu_scr, rstd_scr, *, alpha, nk):
    k = pl.program_id(1)

    @pl.when(k == 0)
    def _():
        o32_ref[...] = alpha * h_ref[...]

    a = a_ref[...]
    tm, d = o32_ref.shape
    nch = min(d, LN_NCHUNK)
    for n0 in range(0, d, nch):
        o32_ref[:, n0:n0 + nch] += _dot(a, b_ref[:, n0:n0 + nch])

    @pl.when(k == nk - 1)
    def _():
        rb = min(tm, LN_ROWS)
        cols = [slice(c, c + LANES) for c in range(0, d, LANES)]

        def row_groups(fn):
            def body(r, carry):
                fn(pl.ds(pl.multiple_of(r * rb, rb), rb))
                return carry
            lax.fori_loop(0, tm // rb, body, 0)

        def lane_mean(part):
            total = jnp.sum(part, axis=1, keepdims=True) * (1.0 / d)
            return jnp.broadcast_to(total, part.shape)

        def tree_sum(xs):
            while len(xs) > 1:
                xs = [xs[n] + xs[n + 1] for n in range(0, len(xs) - 1, 2)] + xs[len(xs) & ~1:]
            return xs[0]

        def pass_sum(rows):
            part_scr[rows, :] = tree_sum([o32_ref[rows, c] for c in cols])

        def pass_sq(rows):
            mu = mu_scr[rows, :]
            dev = [o32_ref[rows, c] - mu for c in cols]
            part_scr[rows, :] = tree_sum([x * x for x in dev])

        def pass_norm(rows):
            mu = mu_scr[rows, :]
            rstd = rstd_scr[rows, :]
            for c in cols:
                out = (o32_ref[rows, c] - mu) * rstd * g_ref[:, c] + beta_ref[:, c]
                o32_ref[rows, c] = out
                obf_ref[rows, c] = out.astype(BF16)

        row_groups(pass_sum)
        mu_scr[...] = lane_mean(part_scr[...])
        row_groups(pass_sq)
        rstd_scr[...] = lax.rsqrt(lane_mean(part_scr[...]) + LN_EPS)
        row_groups(pass_norm)


def _mm_res_ln(a, b, h32, g, beta, alpha):
    m, kdim = a.shape
    d = b.shape[1]
    tm, tk = _tile(m, LN_TM), _tile(kdim, LN_TK)
    nk = kdim // tk
    return pl.pallas_call(
        functools.partial(_mm_res_ln_kernel, alpha=alpha, nk=nk),
        grid=(m // tm, nk),
        in_specs=[pl.BlockSpec((tm, tk), lambda i, k: (i, k)),
                  pl.BlockSpec((tk, d), lambda i, k: (k, 0)),
                  pl.BlockSpec((tm, d), lambda i, k: (i, 0)),
                  pl.BlockSpec((1, d), lambda i, k: (0, 0)),
                  pl.BlockSpec((1, d), lambda i, k: (0, 0))],
        out_specs=[pl.BlockSpec((tm, d), lambda i, k: (i, 0)),
                   pl.BlockSpec((tm, d), lambda i, k: (i, 0))],
        out_shape=[jax.ShapeDtypeStruct((m, d), F32), jax.ShapeDtypeStruct((m, d), BF16)],
        scratch_shapes=[pltpu.VMEM((tm, LANES), F32)] * 3,
        compiler_params=_params("parallel", "arbitrary"),
        name="proj_res_ln",
    )(a, b, h32, g.reshape(1, d).astype(F32), beta.reshape(1, d).astype(F32))


def _ffn_up_kernel(x_ref, wg_ref, wu_ref, o_ref, *, n_real):
    j = pl.program_id(1)

    @pl.when(j < n_real)
    def _():
        x = x_ref[...]
        gate = _dot(x, wg_ref[...].astype(BF16))
        up = _dot(x, wu_ref[...].astype(BF16))
        o_ref[...] = (gate * _sigmoid(gate) * up).astype(o_ref.dtype)

    @pl.when(j >= n_real)
    def _():
        o_ref[...] = jnp.zeros_like(o_ref)


def _ffn_up(x, w_up_stack, idx, ffn_pad):
    m, d = x.shape
    f = w_up_stack.shape[2] // 2
    tm, tn = _tile(m, BIG_TM), _tile(f, FFN_TN)
    assert ffn_pad % tn == 0
    n_real = f // tn
    real = lambda j: jnp.minimum(j, n_real - 1)
    return pl.pallas_call(
        functools.partial(_ffn_up_kernel, n_real=n_real),
        grid=(m // tm, ffn_pad // tn),
        in_specs=[pl.BlockSpec((tm, d), lambda i, j: (i, 0), pipeline_mode=pl.Buffered(1)),
                  pl.BlockSpec((None, d, tn), lambda i, j: (idx, 0, real(j))),
                  pl.BlockSpec((None, d, tn), lambda i, j: (idx, 0, n_real + real(j)))],
        out_specs=pl.BlockSpec((tm, tn), lambda i, j: (i, j)),
        out_shape=jax.ShapeDtypeStruct((m, ffn_pad), BF16),
        compiler_params=_params("parallel", "arbitrary"),
        name="ffn_up",
    )(x, w_up_stack, w_up_stack)


def _mlstm_kernel(q_ref, k_ref, v_ref, o_ref, gc_ref, gri_ref, grf_ref, ng_ref, out_ref,
                  ct_ref, m_ref, *, chunk, dk, dv, heads, hps):
    hp = pl.program_id(1)
    c = pl.program_id(2)
    L = chunk

    @pl.when(c == 0)
    def _():
        ct_ref[...] = jnp.zeros_like(ct_ref)
        m_ref[...] = jnp.zeros_like(m_ref)

    gc = gc_ref[...]
    lane = lax.broadcasted_iota(jnp.int32, (L, LANES), 1)
    ti = lax.broadcasted_iota(jnp.int32, (L, L), 0)
    si = lax.broadcasted_iota(jnp.int32, (L, L), 1)
    causal = si <= ti
    tri_lo = causal.astype(BF16)
    tri_up = (ti <= si).astype(BF16)
    one_lane = (lane == 0).astype(F32)

    for hl in range(hps):
        h = hp * hps + hl
        kcols = slice(hl * dk, (hl + 1) * dk)
        vcols = slice(hl * dv, (hl + 1) * dv)

        ig_c = jnp.sum(jnp.where(lane == h, gc, 0.0), axis=1, keepdims=True)
        fg_c = jnp.sum(jnp.where(lane == heads + h, gc, 0.0), axis=1, keepdims=True)
        ig_r = gri_ref[pl.ds(h, 1), :]
        fg_r = grf_ref[pl.ds(h, 1), :]
        ig_c, ig_r = _soft_cap(ig_c), _soft_cap(ig_r)
        lf_c, lf_r = _log_sigmoid(_soft_cap(fg_c)), _log_sigmoid(_soft_cap(fg_r))

        b_c = sum(_dot(tri_lo, p) for p in _split3(jnp.broadcast_to(lf_c, (L, LANES))))[:, :1]
        b_r = sum(_dot(p, tri_up) for p in _split3(jnp.broadcast_to(lf_r, (SUBLANES, L))))[:1, :]

        m_prev = m_ref[hl]
        d_log = jnp.where(causal, b_c - b_r + ig_r, NEG_BIG)
        g_log = b_c + m_prev
        m_t = jnp.maximum(g_log, jnp.max(d_log, axis=1, keepdims=True))
        w_intra = jnp.exp(d_log - m_t)
        w_inter = jnp.exp(g_log - m_t)

        q = q_ref[:, kcols]
        k = k_ref[:, kcols]
        v = v_ref[:, vcols]
        s = _dot_nt(q, k) * w_intra
        ct = ct_ref[hl]
        inter = _dot(q, ct.astype(BF16))
        num = _dot(s.astype(BF16), v) + w_inter * inter[:, :dv]
        den = jnp.sum(s, axis=1, keepdims=True) + w_inter * inter[:, dv:dv + 1]
        hh = num * (1.0 / jnp.maximum(jnp.abs(den), jnp.exp(-m_t)))

        b_end = b_c[L - 1:L, :]
        a_c = b_end - b_c + ig_c
        a_r = b_end - b_r + ig_r
        m_new = jnp.maximum(b_end + m_prev, jnp.max(a_r, axis=1, keepdims=True))
        decay = jnp.exp(b_end + m_prev - m_new)
        wa_c = jnp.exp(a_c - m_new)
        wv = jnp.concatenate([wa_c * v.astype(F32), wa_c * one_lane], axis=1).astype(BF16)
        ct_ref[hl] = decay * ct + _dot_tn(k, wv)
        m_ref[hl] = m_new

        hn = hh * lax.rsqrt(jnp.mean(hh * hh, axis=1, keepdims=True) + HEAD_NORM_EPS)
        out_ref[:, vcols] = (hn * ng_ref[:, vcols] * _sigmoid(o_ref[:, vcols])).astype(out_ref.dtype)


def _mlstm_core(qkv, o, gates, gates_t, norm_g, batch, seq):
    heads = MLSTM_HEADS
    assert heads == SUBLANES, "row-form gate blocks assume one sublane tile per gate kind"
    m = qkv.shape[0]
    hv = o.shape[1]
    dv = hv // heads
    dk = (qkv.shape[1] - hv) // (2 * heads)
    assert 2 * heads * dk == hv, "v blocks are addressed in units of dv"
    hps = MLSTM_HEADS_PER_STEP
    assert heads % hps == 0
    groups = heads // hps
    chunk = _tile(seq, MLSTM_CHUNK)
    nc = seq // chunk
    row = lambda b, h, c: b * nc + c
    return pl.pallas_call(
        functools.partial(_mlstm_kernel, chunk=chunk, dk=dk, dv=dv, heads=heads, hps=hps),
        grid=(batch, groups, nc),
        in_specs=[pl.BlockSpec((chunk, hps * dk), lambda b, h, c: (row(b, h, c), h)),
                  pl.BlockSpec((chunk, hps * dk), lambda b, h, c: (row(b, h, c), groups + h)),
                  pl.BlockSpec((chunk, hps * dv), lambda b, h, c: (row(b, h, c), groups + h)),
                  pl.BlockSpec((chunk, hps * dv), lambda b, h, c: (row(b, h, c), h)),
                  pl.BlockSpec((chunk, LANES), lambda b, h, c: (row(b, h, c), 0)),
                  pl.BlockSpec((SUBLANES, chunk), lambda b, h, c: (0, row(b, h, c))),
                  pl.BlockSpec((SUBLANES, chunk), lambda b, h, c: (1, row(b, h, c))),
                  pl.BlockSpec((1, hps * dv), lambda b, h, c: (0, h))],
        out_specs=pl.BlockSpec((chunk, hps * dv), lambda b, h, c: (row(b, h, c), h)),
        out_shape=jax.ShapeDtypeStruct((m, hv), BF16),
        scratch_shapes=[pltpu.VMEM((hps, dk, dv + LANES), F32), pltpu.VMEM((hps, 1, 1), F32)],
        compiler_params=_params("parallel", "parallel", "arbitrary"),
        name="mlstm_core",
    )(qkv, qkv, qkv, o, gates, gates_t, gates_t, norm_g.reshape(1, hv).astype(F32))


def _fox_cum_kernel(g_ref, cum_ref, carry_ref, *, ts):
    @pl.when(pl.program_id(1) == 0)
    def _():
        carry_ref[...] = jnp.zeros_like(carry_ref)

    logf = _log_sigmoid(g_ref[...])
    ti = lax.broadcasted_iota(jnp.int32, (ts, ts), 0)
    si = lax.broadcasted_iota(jnp.int32, (ts, ts), 1)
    tri_lo = (si <= ti).astype(BF16)
    cum = sum(_dot(tri_lo, p) for p in _split3(logf)) + carry_ref[...]
    carry_ref[...] = cum[ts - 1:ts, :]
    cum_ref[...] = cum


def _fox_cumsum(gates, batch, seq):
    ts = _tile(seq, FOX_TS)
    ns = seq // ts
    return pl.pallas_call(
        functools.partial(_fox_cum_kernel, ts=ts),
        grid=(batch, ns),
        in_specs=[pl.BlockSpec((ts, LANES), lambda b, s: (b * ns + s, 0))],
        out_specs=pl.BlockSpec((ts, LANES), lambda b, s: (b * ns + s, 0)),
        out_shape=jax.ShapeDtypeStruct((batch * seq, LANES), F32),
        scratch_shapes=[pltpu.VMEM((1, LANES), F32)],
        compiler_params=_params("parallel", "arbitrary"),
        name="fox_cumsum",
    )(gates)


def _bias_lanes(col, own_first):
    n = col.shape[0]
    hi, mid, lo = (p.astype(F32) for p in _split3(col))
    lane = lax.broadcasted_iota(jnp.int32, (n, LANES), 1)
    base = 0 if own_first else 3
    ones = ((lane >= 3 - base) & (lane < 6 - base)).astype(F32)
    out = jnp.where(lane == base, hi, jnp.where(lane == base + 1, mid, jnp.where(lane == base + 2, lo, ones)))
    return out.astype(BF16)


def _fox_attn_kernel(q_ref, k_ref, v_ref, cc_ref, o_ref, ka_scr, vt_scr, s_scr, p_scr,
                     a_scr, m_scr, l_scr, acc_scr, *, tq, seq, hps):
    hp = pl.program_id(1)
    i = pl.program_id(2)
    dh = LANES
    local_heads = range(hps)

    def head_col(rows, hl):
        lane = lax.broadcasted_iota(jnp.int32, rows.shape, 1)
        return jnp.sum(jnp.where(lane == hp * hps + hl, rows, 0.0), axis=1, keepdims=True) * LOG2E

    @pl.when(i == 0)
    def _():
        for hl in local_heads:
            hc = slice(hl * dh, (hl + 1) * dh)
            for r0 in range(0, seq, tq):
                ka_scr[hl, r0:r0 + tq, :LANES] = k_ref[r0:r0 + tq, hc]
                ka_scr[hl, r0:r0 + tq, LANES:] = _bias_lanes(-head_col(cc_ref[r0:r0 + tq, :], hl), True)
                vt_scr[hl, :, r0:r0 + tq] = v_ref[r0:r0 + tq, hc].astype(F32).T.astype(BF16)

    q0 = pl.multiple_of(i * tq, tq)
    cum_q = cc_ref[pl.ds(q0, tq), :]
    qa = [jnp.concatenate([q_ref[:, hl * dh:(hl + 1) * dh], _bias_lanes(head_col(cum_q, hl), False)], axis=1)
          for hl in local_heads]

    m_scr[...] = jnp.full_like(m_scr, NEG_BIG)
    l_scr[...] = jnp.zeros_like(l_scr)
    acc_scr[...] = jnp.zeros_like(acc_scr)
    a_scr[...] = jnp.ones_like(a_scr)
    for hl in local_heads:
        p_scr[hl, 1] = jnp.zeros((tq, tq), BF16)

    def logits(j, parity):
        start = pl.multiple_of(j * tq, tq)
        for hl in local_heads:
            s_scr[hl, parity] = _dot_nt(ka_scr[hl, pl.ds(start, tq), :], qa[hl])

    def softmax(parity, on_diagonal):
        for hl in local_heads:
            st = s_scr[hl, parity]
            if on_diagonal:
                ki = lax.broadcasted_iota(jnp.int32, (tq, tq), 0)
                qi = lax.broadcasted_iota(jnp.int32, (tq, tq), 1)
                st = jnp.where(ki <= qi, st, NEG_BIG)
            m_old = m_scr[hl]
            m_new = jnp.maximum(m_old, jnp.max(st, axis=0, keepdims=True))
            p = jnp.exp2(st - m_new)
            a = jnp.exp2(m_old - m_new)
            l_scr[hl] = a * l_scr[hl] + jnp.sum(p, axis=0, keepdims=True)
            m_scr[hl] = m_new
            p_scr[hl, parity] = p.astype(BF16)
            a_scr[hl] = a

    def values(j, parity):
        start = pl.multiple_of(jnp.maximum(j, 0) * tq, tq)
        for hl in local_heads:
            acc_scr[hl] = a_scr[hl] * acc_scr[hl] + _dot(vt_scr[hl, :, pl.ds(start, tq)], p_scr[hl, parity])

    def step(j, parity):
        values(j - 1, 1 - parity)
        logits(j + 1, 1 - parity)
        softmax(parity, False)

    def finish(parity):
        values(i - 1, 1 - parity)
        softmax(parity, True)
        values(i, parity)

    logits(0, 0)

    def pair(t, carry):
        step(2 * t, 0)
        step(2 * t + 1, 1)
        return carry

    lax.fori_loop(0, i // 2, pair, 0)

    @pl.when(i % 2 == 0)
    def _():
        finish(0)

    @pl.when(i % 2 == 1)
    def _():
        step(i - 1, 0)
        finish(1)

    for hl in local_heads:
        o_ref[:, hl * dh:(hl + 1) * dh] = (acc_scr[hl] * (1.0 / l_scr[hl])).T.astype(o_ref.dtype)


def _fox_attn(qkv, cum, batch, seq):
    dh = FOX_HEAD_DIM
    assert dh == LANES
    m = qkv.shape[0]
    heads = qkv.shape[1] // (3 * dh)
    assert heads <= LANES
    hps = FOX_HEADS_PER_STEP if heads % FOX_HEADS_PER_STEP == 0 else 1
    groups = heads // hps
    tq = _tile(seq, FOX_TQ)
    nq = seq // tq
    return pl.pallas_call(
        functools.partial(_fox_attn_kernel, tq=tq, seq=seq, hps=hps),
        grid=(batch, groups, nq),
        in_specs=[pl.BlockSpec((tq, hps * dh), lambda b, h, i: (b * nq + i, h)),
                  pl.BlockSpec((seq, hps * dh), lambda b, h, i: (b, groups + h)),
                  pl.BlockSpec((seq, hps * dh), lambda b, h, i: (b, 2 * groups + h)),
                  pl.BlockSpec((seq, LANES), lambda b, h, i: (b, 0))],
        out_specs=pl.BlockSpec((tq, hps * dh), lambda b, h, i: (b * nq + i, h)),
        out_shape=jax.ShapeDtypeStruct((m, heads * dh), BF16),
        scratch_shapes=[pltpu.VMEM((hps, seq, 2 * LANES), BF16), pltpu.VMEM((hps, dh, seq), BF16),
                        pltpu.VMEM((hps, 2, tq, tq), F32), pltpu.VMEM((hps, 2, tq, tq), BF16),
                        pltpu.VMEM((hps, 1, tq), F32), pltpu.VMEM((hps, 1, tq), F32),
                        pltpu.VMEM((hps, 1, tq), F32), pltpu.VMEM((hps, dh, tq), F32)],
        compiler_params=_params("parallel", "parallel", "arbitrary"),
        name="fox_attn",
    )(qkv, qkv, qkv, cum)


def kernel(x, mlstm_w_in, mlstm_b_gate, mlstm_norm_g, mlstm_w_out, fox_w_in, fox_b_f, fox_w_out,
           ln_mix_g, ln_mix_b, ffn_w_up, ffn_w_down, ln_ffn_g, ln_ffn_b):
    batch, seq, d = x.shape
    depth = ln_mix_g.shape[0]
    alpha = (2 * depth) ** 0.25
    m = batch * seq
    ffn = ffn_w_down.shape[1]
    ffn_pad = -(-ffn // FFN_PAD) * FFN_PAD
    hv = mlstm_w_out.shape[1]
    mlstm_main = mlstm_w_in.shape[2] - 2 * MLSTM_HEADS
    mlstm_qk = (mlstm_main - 2 * hv) // 2
    mlstm_dk = mlstm_qk // MLSTM_HEADS
    fox_w = fox_w_out.shape[1]

    mlstm_wt = jnp.swapaxes(mlstm_w_in, 1, 2)
    fox_wt = jnp.swapaxes(fox_w_in, 1, 2)

    h32 = x.reshape(m, d)
    hbf = h32.astype(BF16)
    for layer in range(depth):
        slot = layer // N_MIXERS
        if layer % N_MIXERS == 0:
            qkv = _proj(hbf, mlstm_wt, slot, 0, mlstm_main - hv, BF16, mlstm_qk, mlstm_dk ** -0.5)
            o = _proj(hbf, mlstm_wt, slot, mlstm_main - hv, hv, F32)
            gates = _gate_proj(h32, mlstm_wt, slot, mlstm_main, mlstm_b_gate[slot])
            mixed = _mlstm_core(qkv, o, gates, gates.T, mlstm_norm_g[slot], batch, seq)
            w_out = _cast_pad_rows(mlstm_w_out, slot, hv)
        else:
            qkv = _proj(hbf, fox_wt, slot, 0, 3 * fox_w, BF16, fox_w, FOX_HEAD_DIM ** -0.5 * LOG2E)
            gates = _gate_proj(h32, fox_wt, slot, 3 * fox_w, fox_b_f[slot])
            mixed = _fox_attn(qkv, _fox_cumsum(gates, batch, seq), batch, seq)
            w_out = _cast_pad_rows(fox_w_out, slot, fox_w)
        h32, hbf = _mm_res_ln(mixed, w_out, h32, ln_mix_g[layer], ln_mix_b[layer], alpha)

        hidden = _ffn_up(hbf, ffn_w_up, layer, ffn_pad)
        w_down = _cast_pad_rows(ffn_w_down, layer, ffn_pad)
        h32, hbf = _mm_res_ln(hidden, w_down, h32, ln_ffn_g[layer], ln_ffn_b[layer], alpha)
    return h32.reshape(batch, seq, d)
```

```python
import functools
import math

import jax
import jax.numpy as jnp
from jax import lax
from jax.experimental import pallas as pl
from jax.experimental.pallas import tpu as pltpu

F32 = jnp.float32
BF16 = jnp.bfloat16

MLSTM_HEADS = 8
FOX_HEAD_DIM = 128
GATE_SOFT_CAP = 15.0
HEAD_NORM_EPS = 1e-6
LN_EPS = 1e-5
N_MIXERS = 2

LANES = 128
SUBLANES = 8
NEG_BIG = -1e30
LOG2E = math.log2(math.e)

BIG_TM = 2048
MM_TN = 512
FFN_TN = 256
LN_TM = 512
LN_TK = 1024
LN_H_CHUNKS = 8
LN_NCHUNK = 1024
LN_ROWS = 16
GATE_TM = 512
CAST_ROWS = 256
FFN_PAD = 1024
MLSTM_CHUNK = 256
MLSTM_HEADS_PER_STEP = 4
FOX_TQ = 512
FOX_TS = 512
FOX_HEADS_PER_STEP = 4
VMEM_LIMIT = 56 * 1024 * 1024


def _tile(n, pref):
    if n <= pref:
        return n
    t = (pref // LANES) * LANES
    while t >= LANES:
        if n % t == 0:
            return t
        t -= LANES
    raise ValueError(f"no lane-aligned tile of {n} below {pref}")


def _params(*sem):
    return pltpu.CompilerParams(dimension_semantics=sem, vmem_limit_bytes=VMEM_LIMIT)


def _dot(a, b):
    return jnp.dot(a, b, preferred_element_type=F32)


def _dot_nt(a, b):
    return lax.dot_general(a, b, (((1,), (1,)), ((), ())), preferred_element_type=F32)


def _dot_tn(a, b):
    return lax.dot_general(a, b, (((0,), (0,)), ((), ())), preferred_element_type=F32)


def _split3(x):
    hi = x.astype(BF16)
    r1 = x - hi.astype(F32)
    mid = r1.astype(BF16)
    lo = (r1 - mid.astype(F32)).astype(BF16)
    return hi, mid, lo


def _sigmoid(x):
    return 1.0 / (1.0 + jnp.exp(-x))


def _log_sigmoid(x):
    return jnp.minimum(x, 0.0) - jnp.log1p(jnp.exp(-jnp.abs(x)))


def _soft_cap(z):
    return GATE_SOFT_CAP * jnp.tanh(z / GATE_SOFT_CAP)


def _cast_kernel(w_ref, o_ref, *, n_real):
    r = pl.program_id(0)

    @pl.when(r < n_real)
    def _():
        o_ref[...] = w_ref[...].astype(BF16)

    @pl.when(r >= n_real)
    def _():
        o_ref[...] = jnp.zeros_like(o_ref)


def _cast_pad_rows(w_stack, idx, rows_pad):
    _, rows, cols = w_stack.shape
    tr = _tile(rows, CAST_ROWS)
    assert rows_pad % tr == 0
    n_real = rows // tr
    return pl.pallas_call(
        functools.partial(_cast_kernel, n_real=n_real),
        grid=(rows_pad // tr,),
        in_specs=[pl.BlockSpec((None, tr, cols), lambda r: (idx, jnp.minimum(r, n_real - 1), 0))],
        out_specs=pl.BlockSpec((tr, cols), lambda r: (r, 0)),
        out_shape=jax.ShapeDtypeStruct((rows_pad, cols), BF16),
        compiler_params=_params("parallel"),
        name="cast_w",
    )(w_stack)


def _proj_kernel(x_ref, w_ref, o_ref, *, n_scaled, scale):
    j = pl.program_id(1)
    acc = _dot_nt(x_ref[...], w_ref[...].astype(BF16))
    if n_scaled:
        acc = acc * jnp.where(j < n_scaled, scale, 1.0)
    o_ref[...] = acc.astype(o_ref.dtype)


def _proj(x, wt_stack, idx, col0, ncols, out_dtype, scaled_cols=0, scale=1.0):
    m, k = x.shape
    tm, tn = _tile(m, BIG_TM), _tile(ncols, MM_TN)
    assert col0 % tn == 0 and scaled_cols % tn == 0
    j0 = col0 // tn
    return pl.pallas_call(
        functools.partial(_proj_kernel, n_scaled=scaled_cols // tn, scale=scale),
        grid=(m // tm, ncols // tn),
        in_specs=[pl.BlockSpec((tm, k), lambda i, j: (i, 0), pipeline_mode=pl.Buffered(1)),
                  pl.BlockSpec((None, tn, k), lambda i, j: (idx, j0 + j, 0))],
        out_specs=pl.BlockSpec((tm, tn), lambda i, j: (i, j)),
        out_shape=jax.ShapeDtypeStruct((m, ncols), out_dtype),
        compiler_params=_params("parallel", "arbitrary"),
        name="proj",
    )(x, wt_stack)


def _gate_kernel(h_ref, w_ref, b_ref, o_ref, *, n_gates):
    x = h_ref[...]
    w = w_ref[...]
    x_hi = x.astype(BF16)
    x_lo = (x - x_hi.astype(F32)).astype(BF16)
    w_hi = w.astype(BF16)
    w_lo = (w - w_hi.astype(F32)).astype(BF16)
    acc = _dot_nt(x_hi, w_hi) + _dot_nt(x_lo, w_hi) + _dot_nt(x_hi, w_lo)
    lane = lax.broadcasted_iota(jnp.int32, acc.shape, 1)
    o_ref[...] = jnp.where(lane < n_gates, acc + b_ref[...], 0.0)


def _gate_proj(h32, wt_stack, idx, col0, bias):
    m, d = h32.shape
    g = wt_stack.shape[1] - col0
    assert col0 % LANES == 0 and 0 < g <= LANES
    b_pad = jnp.pad(bias.astype(F32), (0, LANES - g)).reshape(1, LANES)
    tm = _tile(m, GATE_TM)
    return pl.pallas_call(
        functools.partial(_gate_kernel, n_gates=g),
        grid=(m // tm,),
        in_specs=[pl.BlockSpec((tm, d), lambda i: (i, 0)),
                  pl.BlockSpec((None, LANES, d), lambda i: (idx, col0 // LANES, 0)),
                  pl.BlockSpec((1, LANES), lambda i: (0, 0))],
        out_specs=pl.BlockSpec((tm, LANES), lambda i: (i, 0)),
        out_shape=jax.ShapeDtypeStruct((m, LANES), F32),
        compiler_params=_params("parallel"),
        name="gate_proj",
    )(h32, wt_stack, b_pad)


def _mm_res_ln_kernel(a_ref, b_ref, h_ref, g_ref, beta_ref, o32_ref, obf_ref,
                      part_scr, mu_scr, rstd_scr, *, alpha, nk, nh):
    k = pl.program_id(1)
    a = a_ref[...]
    tm, d = o32_ref.shape
    nch = min(d, LN_NCHUNK)
    chunks = range(0, d, nch)

    @pl.when(k == 0)
    def _():
        for n0 in chunks:
            o32_ref[:, n0:n0 + nch] = _dot(a, b_ref[:, n0:n0 + nch])

    @pl.when(k > 0)
    def _():
        for n0 in chunks:
            o32_ref[:, n0:n0 + nch] += _dot(a, b_ref[:, n0:n0 + nch])

    @pl.when(k < nh)
    def _():
        hc = d // nh
        cols = pl.ds(pl.multiple_of(k * hc, hc), hc)
        o32_ref[:, cols] += alpha * h_ref[...]

    @pl.when(k == nk - 1)
    def _():
        rb = min(tm, LN_ROWS)
        cols = [slice(c, c + LANES) for c in range(0, d, LANES)]

        def row_groups(fn):
            def body(r, carry):
                fn(pl.ds(pl.multiple_of(r * rb, rb), rb))
                return carry
            lax.fori_loop(0, tm // rb, body, 0)

        def lane_mean(part):
            total = jnp.sum(part, axis=1, keepdims=True) * (1.0 / d)
            return jnp.broadcast_to(total, part.shape)

        def tree_sum(xs):
            while len(xs) > 1:
                xs = [xs[n] + xs[n + 1] for n in range(0, len(xs) - 1, 2)] + xs[len(xs) & ~1:]
            return xs[0]

        def pass_sum(rows):
            part_scr[rows, :] = tree_sum([o32_ref[rows, c] for c in cols])

        def pass_sq(rows):
            mu = mu_scr[rows, :]
            dev = [o32_ref[rows, c] - mu for c in cols]
            part_scr[rows, :] = tree_sum([x * x for x in dev])

        def pass_norm(rows):
            mu = mu_scr[rows, :]
            rstd = rstd_scr[rows, :]
            for c in cols:
                out = (o32_ref[rows, c] - mu) * rstd * g_ref[:, c] + beta_ref[:, c]
                o32_ref[rows, c] = out
                obf_ref[rows, c] = out.astype(BF16)

        row_groups(pass_sum)
        mu_scr[...] = lane_mean(part_scr[...])
        row_groups(pass_sq)
        rstd_scr[...] = lax.rsqrt(lane_mean(part_scr[...]) + LN_EPS)
        row_groups(pass_norm)


def _mm_res_ln(a, b, h32, g, beta, alpha):
    m, kdim = a.shape
    d = b.shape[1]
    tm, tk = _tile(m, LN_TM), _tile(kdim, LN_TK)
    nk = kdim // tk
    nh = max(n for n in range(1, min(nk, LN_H_CHUNKS) + 1) if d % (n * LANES) == 0)
    return pl.pallas_call(
        functools.partial(_mm_res_ln_kernel, alpha=alpha, nk=nk, nh=nh),
        grid=(m // tm, nk),
        in_specs=[pl.BlockSpec((tm, tk), lambda i, k: (i, k)),
                  pl.BlockSpec((tk, d), lambda i, k: (k, 0)),
                  pl.BlockSpec((tm, d // nh), lambda i, k: (i, jnp.minimum(k, nh - 1))),
                  pl.BlockSpec((1, d), lambda i, k: (0, 0)),
                  pl.BlockSpec((1, d), lambda i, k: (0, 0))],
        out_specs=[pl.BlockSpec((tm, d), lambda i, k: (i, 0)),
                   pl.BlockSpec((tm, d), lambda i, k: (i, 0))],
        out_shape=[jax.ShapeDtypeStruct((m, d), F32), jax.ShapeDtypeStruct((m, d), BF16)],
        scratch_shapes=[pltpu.VMEM((tm, LANES), F32)] * 3,
        compiler_params=_params("parallel", "arbitrary"),
        name="proj_res_ln",
    )(a, b, h32, g.reshape(1, d).astype(F32), beta.reshape(1, d).astype(F32))


def _ffn_up_kernel(x_ref, wg_ref, wu_ref, o_ref, *, n_real):
    j = pl.program_id(1)

    @pl.when(j < n_real)
    def _():
        x = x_ref[...]
        gate = _dot(x, wg_ref[...].astype(BF16))
        up = _dot(x, wu_ref[...].astype(BF16))
        o_ref[...] = (gate * _sigmoid(gate) * up).astype(o_ref.dtype)

    @pl.when(j >= n_real)
    def _():
        o_ref[...] = jnp.zeros_like(o_ref)


def _ffn_up(x, w_up_stack, idx, ffn_pad):
    m, d = x.shape
    f = w_up_stack.shape[2] // 2
    tm, tn = _tile(m, BIG_TM), _tile(f, FFN_TN)
    assert ffn_pad % tn == 0
    n_real = f // tn
    real = lambda j: jnp.minimum(j, n_real - 1)
    return pl.pallas_call(
        functools.partial(_ffn_up_kernel, n_real=n_real),
        grid=(m // tm, ffn_pad // tn),
        in_specs=[pl.BlockSpec((tm, d), lambda i, j: (i, 0), pipeline_mode=pl.Buffered(1)),
                  pl.BlockSpec((None, d, tn), lambda i, j: (idx, 0, real(j))),
                  pl.BlockSpec((None, d, tn), lambda i, j: (idx, 0, n_real + real(j)))],
        out_specs=pl.BlockSpec((tm, tn), lambda i, j: (i, j)),
        out_shape=jax.ShapeDtypeStruct((m, ffn_pad), BF16),
        compiler_params=_params("parallel", "arbitrary"),
        name="ffn_up",
    )(x, w_up_stack, w_up_stack)


def _mlstm_kernel(q_ref, k_ref, v_ref, o_ref, gc_ref, gri_ref, grf_ref, ng_ref, out_ref,
                  ct_ref, m_ref, *, chunk, dk, dv, heads, hps):
    hp = pl.program_id(1)
    c = pl.program_id(2)
    L = chunk

    @pl.when(c == 0)
    def _():
        ct_ref[...] = jnp.zeros_like(ct_ref)
        m_ref[...] = jnp.zeros_like(m_ref)

    gc = gc_ref[...]
    lane = lax.broadcasted_iota(jnp.int32, (L, LANES), 1)
    ti = lax.broadcasted_iota(jnp.int32, (L, L), 0)
    si = lax.broadcasted_iota(jnp.int32, (L, L), 1)
    causal = si <= ti
    tri_lo = causal.astype(BF16)
    tri_up = (ti <= si).astype(BF16)
    one_lane = (lane == 0).astype(F32)

    for hl in range(hps):
        h = hp * hps + hl
        kcols = slice(hl * dk, (hl + 1) * dk)
        vcols = slice(hl * dv, (hl + 1) * dv)

        ig_c = jnp.sum(jnp.where(lane == h, gc, 0.0), axis=1, keepdims=True)
        fg_c = jnp.sum(jnp.where(lane == heads + h, gc, 0.0), axis=1, keepdims=True)
        ig_r = gri_ref[pl.ds(h, 1), :]
        fg_r = grf_ref[pl.ds(h, 1), :]
        ig_c, ig_r = _soft_cap(ig_c), _soft_cap(ig_r)
        lf_c, lf_r = _log_sigmoid(_soft_cap(fg_c)), _log_sigmoid(_soft_cap(fg_r))

        b_c = sum(_dot(tri_lo, p) for p in _split3(jnp.broadcast_to(lf_c, (L, LANES))))[:, :1]
        b_r = sum(_dot(p, tri_up) for p in _split3(jnp.broadcast_to(lf_r, (SUBLANES, L))))[:1, :]

        m_prev = m_ref[hl]
        d_log = jnp.where(causal, b_c - b_r + ig_r, NEG_BIG)
        g_log = b_c + m_prev
        m_t = jnp.maximum(g_log, jnp.max(d_log, axis=1, keepdims=True))
        w_intra = jnp.exp(d_log - m_t)
        w_inter = jnp.exp(g_log - m_t)

        q = q_ref[:, kcols]
        k = k_ref[:, kcols]
        v = v_ref[:, vcols]
        s = _dot_nt(q, k) * w_intra
        ct = ct_ref[hl]
        inter = _dot(q, ct.astype(BF16))
        num = _dot(s.astype(BF16), v) + w_inter * inter[:, :dv]
        den = jnp.sum(s, axis=1, keepdims=True) + w_inter * inter[:, dv:dv + 1]
        hh = num * (1.0 / jnp.maximum(jnp.abs(den), jnp.exp(-m_t)))

        b_end = b_c[L - 1:L, :]
        a_c = b_end - b_c + ig_c
        a_r = b_end - b_r + ig_r
        m_new = jnp.maximum(b_end + m_prev, jnp.max(a_r, axis=1, keepdims=True))
        decay = jnp.exp(b_end + m_prev - m_new)
        wa_c = jnp.exp(a_c - m_new)
        wv = jnp.concatenate([wa_c * v.astype(F32), wa_c * one_lane], axis=1).astype(BF16)
        ct_ref[hl] = decay * ct + _dot_tn(k, wv)
        m_ref[hl] = m_new

        hn = hh * lax.rsqrt(jnp.mean(hh * hh, axis=1, keepdims=True) + HEAD_NORM_EPS)
        out_ref[:, vcols] = (hn * ng_ref[:, vcols] * _sigmoid(o_ref[:, vcols])).astype(out_ref.dtype)


def _mlstm_core(qkv, o, gates, gates_t, norm_g, batch, seq):
    heads = MLSTM_HEADS
    assert heads == SUBLANES, "row-form gate blocks assume one sublane tile per gate kind"
    m = qkv.shape[0]
    hv = o.shape[1]
    dv = hv // heads
    dk = (qkv.shape[1] - hv) // (2 * heads)
    assert 2 * heads * dk == hv, "v blocks are addressed in units of dv"
    hps = MLSTM_HEADS_PER_STEP
    assert heads % hps == 0
    groups = heads // hps
    chunk = _tile(seq, MLSTM_CHUNK)
    nc = seq // chunk
    row = lambda b, h, c: b * nc + c
    return pl.pallas_call(
        functools.partial(_mlstm_kernel, chunk=chunk, dk=dk, dv=dv, heads=heads, hps=hps),
        grid=(batch, groups, nc),
        in_specs=[pl.BlockSpec((chunk, hps * dk), lambda b, h, c: (row(b, h, c), h)),
                  pl.BlockSpec((chunk, hps * dk), lambda b, h, c: (row(b, h, c), groups + h)),
                  pl.BlockSpec((chunk, hps * dv), lambda b, h, c: (row(b, h, c), groups + h)),
                  pl.BlockSpec((chunk, hps * dv), lambda b, h, c: (row(b, h, c), h)),
                  pl.BlockSpec((chunk, LANES), lambda b, h, c: (row(b, h, c), 0)),
                  pl.BlockSpec((SUBLANES, chunk), lambda b, h, c: (0, row(b, h, c))),
                  pl.BlockSpec((SUBLANES, chunk), lambda b, h, c: (1, row(b, h, c))),
                  pl.BlockSpec((1, hps * dv), lambda b, h, c: (0, h))],
        out_specs=pl.BlockSpec((chunk, hps * dv), lambda b, h, c: (row(b, h, c), h)),
        out_shape=jax.ShapeDtypeStruct((m, hv), BF16),
        scratch_shapes=[pltpu.VMEM((hps, dk, dv + LANES), F32), pltpu.VMEM((hps, 1, 1), F32)],
        compiler_params=_params("parallel", "parallel", "arbitrary"),
        name="mlstm_core",
    )(qkv, qkv, qkv, o, gates, gates_t, gates_t, norm_g.reshape(1, hv).astype(F32))


def _fox_cum_kernel(g_ref, cum_ref, carry_ref, *, ts):
    @pl.when(pl.program_id(1) == 0)
    def _():
        carry_ref[...] = jnp.zeros_like(carry_ref)

    logf = _log_sigmoid(g_ref[...])
    ti = lax.broadcasted_iota(jnp.int32, (ts, ts), 0)
    si = lax.broadcasted_iota(jnp.int32, (ts, ts), 1)
    tri_lo = (si <= ti).astype(BF16)
    cum = sum(_dot(tri_lo, p) for p in _split3(logf)) + carry_ref[...]
    carry_ref[...] = cum[ts - 1:ts, :]
    cum_ref[...] = cum


def _fox_cumsum(gates, batch, seq):
    ts = _tile(seq, FOX_TS)
    ns = seq // ts
    return pl.pallas_call(
        functools.partial(_fox_cum_kernel, ts=ts),
        grid=(batch, ns),
        in_specs=[pl.BlockSpec((ts, LANES), lambda b, s: (b * ns + s, 0))],
        out_specs=pl.BlockSpec((ts, LANES), lambda b, s: (b * ns + s, 0)),
        out_shape=jax.ShapeDtypeStruct((batch * seq, LANES), F32),
        scratch_shapes=[pltpu.VMEM((1, LANES), F32)],
        compiler_params=_params("parallel", "arbitrary"),
        name="fox_cumsum",
    )(gates)


def _bias_lanes(col, own_first):
    n = col.shape[0]
    hi, mid, lo = (p.astype(F32) for p in _split3(col))
    lane = lax.broadcasted_iota(jnp.int32, (n, LANES), 1)
    base = 0 if own_first else 3
    ones = ((lane >= 3 - base) & (lane < 6 - base)).astype(F32)
    out = jnp.where(lane == base, hi, jnp.where(lane == base + 1, mid, jnp.where(lane == base + 2, lo, ones)))
    return out.astype(BF16)


def _fox_attn_kernel(q_ref, k_ref, v_ref, cc_ref, o_ref, ka_scr, vt_scr, s_scr, p_scr,
                     a_scr, m_scr, l_scr, acc_scr, *, tq, seq, hps):
    hp = pl.program_id(1)
    i = pl.program_id(2)
    dh = LANES
    local_heads = range(hps)

    def head_col(rows, hl):
        lane = lax.broadcasted_iota(jnp.int32, rows.shape, 1)
        return jnp.sum(jnp.where(lane == hp * hps + hl, rows, 0.0), axis=1, keepdims=True) * LOG2E

    @pl.when(i == 0)
    def _():
        for hl in local_heads:
            hc = slice(hl * dh, (hl + 1) * dh)
            for r0 in range(0, seq, tq):
                ka_scr[hl, r0:r0 + tq, :LANES] = k_ref[r0:r0 + tq, hc]
                ka_scr[hl, r0:r0 + tq, LANES:] = _bias_lanes(-head_col(cc_ref[r0:r0 + tq, :], hl), True)
                vt_scr[hl, :, r0:r0 + tq] = v_ref[r0:r0 + tq, hc].astype(F32).T.astype(BF16)

    q0 = pl.multiple_of(i * tq, tq)
    cum_q = cc_ref[pl.ds(q0, tq), :]
    qa = [jnp.concatenate([q_ref[:, hl * dh:(hl + 1) * dh], _bias_lanes(head_col(cum_q, hl), False)], axis=1)
          for hl in local_heads]

    m_scr[...] = jnp.full_like(m_scr, NEG_BIG)
    l_scr[...] = jnp.zeros_like(l_scr)
    acc_scr[...] = jnp.zeros_like(acc_scr)
    a_scr[...] = jnp.ones_like(a_scr)
    for hl in local_heads:
        p_scr[hl, 1] = jnp.zeros((tq, tq), BF16)

    def logits(j, parity):
        start = pl.multiple_of(j * tq, tq)
        for hl in local_heads:
            s_scr[hl, parity] = _dot_nt(ka_scr[hl, pl.ds(start, tq), :], qa[hl])

    def softmax(parity, on_diagonal):
        for hl in local_heads:
            st = s_scr[hl, parity]
            if on_diagonal:
                ki = lax.broadcasted_iota(jnp.int32, (tq, tq), 0)
                qi = lax.broadcasted_iota(jnp.int32, (tq, tq), 1)
                st = jnp.where(ki <= qi, st, NEG_BIG)
            m_old = m_scr[hl]
            m_new = jnp.maximum(m_old, jnp.max(st, axis=0, keepdims=True))
            p = jnp.exp2(st - m_new)
            a = jnp.exp2(m_old - m_new)
            l_scr[hl] = a * l_scr[hl] + jnp.sum(p, axis=0, keepdims=True)
            m_scr[hl] = m_new
            p_scr[hl, parity] = p.astype(BF16)
            a_scr[hl] = a

    def values(j, parity):
        start = pl.multiple_of(jnp.maximum(j, 0) * tq, tq)
        for hl in local_heads:
            acc_scr[hl] = a_scr[hl] * acc_scr[hl] + _dot(vt_scr[hl, :, pl.ds(start, tq)], p_scr[hl, parity])

    def step(j, parity):
        values(j - 1, 1 - parity)
        logits(j + 1, 1 - parity)
        softmax(parity, False)

    def finish(parity):
        values(i - 1, 1 - parity)
        softmax(parity, True)
        values(i, parity)

    logits(0, 0)

    def pair(t, carry):
        step(2 * t, 0)
        step(2 * t + 1, 1)
        return carry

    lax.fori_loop(0, i // 2, pair, 0)

    @pl.when(i % 2 == 0)
    def _():
        finish(0)

    @pl.when(i % 2 == 1)
    def _():
        step(i - 1, 0)
        finish(1)

    for hl in local_heads:
        o_ref[:, hl * dh:(hl + 1) * dh] = (acc_scr[hl] * (1.0 / l_scr[hl])).T.astype(o_ref.dtype)


def _fox_attn(qkv, cum, batch, seq):
    dh = FOX_HEAD_DIM
    assert dh == LANES
    m = qkv.shape[0]
    heads = qkv.shape[1] // (3 * dh)
    assert heads <= LANES
    hps = FOX_HEADS_PER_STEP if heads % FOX_HEADS_PER_STEP == 0 else 1
    groups = heads // hps
    tq = _tile(seq, FOX_TQ)
    nq = seq // tq
    return pl.pallas_call(
        functools.partial(_fox_attn_kernel, tq=tq, seq=seq, hps=hps),
        grid=(batch, groups, nq),
        in_specs=[pl.BlockSpec((tq, hps * dh), lambda b, h, i: (b * nq + i, h)),
                  pl.BlockSpec((seq, hps * dh), lambda b, h, i: (b, groups + h)),
                  pl.BlockSpec((seq, hps * dh), lambda b, h, i: (b, 2 * groups + h)),
                  pl.BlockSpec((seq, LANES), lambda b, h, i: (b, 0))],
        out_specs=pl.BlockSpec((tq, hps * dh), lambda b, h, i: (b * nq + i, h)),
        out_shape=jax.ShapeDtypeStruct((m, heads * dh), BF16),
        scratch_shapes=[pltpu.VMEM((hps, seq, 2 * LANES), BF16), pltpu.VMEM((hps, dh, seq), BF16),
                        pltpu.VMEM((hps, 2, tq, tq), F32), pltpu.VMEM((hps, 2, tq, tq), BF16),
                        pltpu.VMEM((hps, 1, tq), F32), pltpu.VMEM((hps, 1, tq), F32),
                        pltpu.VMEM((hps, 1, tq), F32), pltpu.VMEM((hps, dh, tq), F32)],
        compiler_params=_params("parallel", "parallel", "arbitrary"),
        name="fox_attn",
    )(qkv, qkv, qkv, cum)


def kernel(x, mlstm_w_in, mlstm_b_gate, mlstm_norm_g, mlstm_w_out, fox_w_in, fox_b_f, fox_w_out,
           ln_mix_g, ln_mix_b, ffn_w_up, ffn_w_down, ln_ffn_g, ln_ffn_b):
    batch, seq, d = x.shape
    depth = ln_mix_g.shape[0]
    alpha = (2 * depth) ** 0.25
    m = batch * seq
    ffn = ffn_w_down.shape[1]
    ffn_pad = -(-ffn // FFN_PAD) * FFN_PAD
    hv = mlstm_w_out.shape[1]
    mlstm_main = mlstm_w_in.shape[2] - 2 * MLSTM_HEADS
    mlstm_qk = (mlstm_main - 2 * hv) // 2
    mlstm_dk = mlstm_qk // MLSTM_HEADS
    fox_w = fox_w_out.shape[1]

    mlstm_wt = jnp.swapaxes(mlstm_w_in, 1, 2)
    fox_wt = jnp.swapaxes(fox_w_in, 1, 2)

    h32 = x.reshape(m, d)
    hbf = h32.astype(BF16)
    for layer in range(depth):
        slot = layer // N_MIXERS
        if layer % N_MIXERS == 0:
            qkv = _proj(hbf, mlstm_wt, slot, 0, mlstm_main - hv, BF16, mlstm_qk, mlstm_dk ** -0.5)
            o = _proj(hbf, mlstm_wt, slot, mlstm_main - hv, hv, F32)
            gates = _gate_proj(h32, mlstm_wt, slot, mlstm_main, mlstm_b_gate[slot])
            mixed = _mlstm_core(qkv, o, gates, gates.T, mlstm_norm_g[slot], batch, seq)
            w_out = _cast_pad_rows(mlstm_w_out, slot, hv)
        else:
            qkv = _proj(hbf, fox_wt, slot, 0, 3 * fox_w, BF16, fox_w, FOX_HEAD_DIM ** -0.5 * LOG2E)
            gates = _gate_proj(h32, fox_wt, slot, 3 * fox_w, fox_b_f[slot])
            mixed = _fox_attn(qkv, _fox_cumsum(gates, batch, seq), batch, seq)
            w_out = _cast_pad_rows(fox_w_out, slot, fox_w)
        h32, hbf = _mm_res_ln(mixed, w_out, h32, ln_mix_g[layer], ln_mix_b[layer], alpha)

        hidden = _ffn_up(hbf, ffn_w_up, layer, ffn_pad)
        w_down = _cast_pad_rows(ffn_w_down, layer, ffn_pad)
        h32, hbf = _mm_res_ln(hidden, w_down, h32, ln_ffn_g[layer], ln_ffn_b[layer], alpha)
    return h32.reshape(batch, seq, d)
```

```python
import functools
import math

import jax
import jax.numpy as jnp
from jax import lax
from jax.experimental import pallas as pl
from jax.experimental.pallas import tpu as pltpu

F32 = jnp.float32
BF16 = jnp.bfloat16

MLSTM_HEADS = 8
FOX_HEAD_DIM = 128
GATE_SOFT_CAP = 15.0
HEAD_NORM_EPS = 1e-6
LN_EPS = 1e-5
N_MIXERS = 2

LANES = 128
SUBLANES = 8
NEG_BIG = -1e30
LOG2E = math.log2(math.e)

BIG_TM = 2048
MM_TN = 512
FFN_TN = 256
LN_TM = 1024
LN_OUT_ROWS = 128
LN_TK = 1024
LN_H_CHUNKS = 8
LN_NCHUNK = 1024
LN_ROWS = 16
GATE_TM = 512
CAST_ROWS = 256
FFN_PAD = 1024
MLSTM_CHUNK = 256
MLSTM_HEADS_PER_STEP = 4
FOX_TQ = 512
FOX_TS = 512
FOX_HEADS_PER_STEP = 4
VMEM_LIMIT = 56 * 1024 * 1024


def _tile(n, pref):
    if n <= pref:
        return n
    t = (pref // LANES) * LANES
    while t >= LANES:
        if n % t == 0:
            return t
        t -= LANES
    raise ValueError(f"no lane-aligned tile of {n} below {pref}")


def _params(*sem):
    return pltpu.CompilerParams(dimension_semantics=sem, vmem_limit_bytes=VMEM_LIMIT)


def _dot(a, b):
    return jnp.dot(a, b, preferred_element_type=F32)


def _dot_nt(a, b):
    return lax.dot_general(a, b, (((1,), (1,)), ((), ())), preferred_element_type=F32)


def _dot_tn(a, b):
    return lax.dot_general(a, b, (((0,), (0,)), ((), ())), preferred_element_type=F32)


def _split3(x):
    hi = x.astype(BF16)
    r1 = x - hi.astype(F32)
    mid = r1.astype(BF16)
    lo = (r1 - mid.astype(F32)).astype(BF16)
    return hi, mid, lo


def _sigmoid(x):
    return 1.0 / (1.0 + jnp.exp(-x))


def _log_sigmoid(x):
    return jnp.minimum(x, 0.0) - jnp.log1p(jnp.exp(-jnp.abs(x)))


def _soft_cap(z):
    return GATE_SOFT_CAP * jnp.tanh(z / GATE_SOFT_CAP)


def _cast_kernel(w_ref, o_ref, *, n_real):
    r = pl.program_id(0)

    @pl.when(r < n_real)
    def _():
        o_ref[...] = w_ref[...].astype(BF16)

    @pl.when(r >= n_real)
    def _():
        o_ref[...] = jnp.zeros_like(o_ref)


def _cast_pad_rows(w_stack, idx, rows_pad):
    _, rows, cols = w_stack.shape
    tr = _tile(rows, CAST_ROWS)
    assert rows_pad % tr == 0
    n_real = rows // tr
    return pl.pallas_call(
        functools.partial(_cast_kernel, n_real=n_real),
        grid=(rows_pad // tr,),
        in_specs=[pl.BlockSpec((None, tr, cols), lambda r: (idx, jnp.minimum(r, n_real - 1), 0))],
        out_specs=pl.BlockSpec((tr, cols), lambda r: (r, 0)),
        out_shape=jax.ShapeDtypeStruct((rows_pad, cols), BF16),
        compiler_params=_params("parallel"),
        name="cast_w",
    )(w_stack)


def _proj_kernel(x_ref, w_ref, o_ref, *, n_scaled, scale):
    j = pl.program_id(1)
    acc = _dot_nt(x_ref[...], w_ref[...].astype(BF16))
    if n_scaled:
        acc = acc * jnp.where(j < n_scaled, scale, 1.0)
    o_ref[...] = acc.astype(o_ref.dtype)


def _proj(x, wt_stack, idx, col0, ncols, out_dtype, scaled_cols=0, scale=1.0):
    m, k = x.shape
    tm, tn = _tile(m, BIG_TM), _tile(ncols, MM_TN)
    assert col0 % tn == 0 and scaled_cols % tn == 0
    j0 = col0 // tn
    return pl.pallas_call(
        functools.partial(_proj_kernel, n_scaled=scaled_cols // tn, scale=scale),
        grid=(m // tm, ncols // tn),
        in_specs=[pl.BlockSpec((tm, k), lambda i, j: (i, 0), pipeline_mode=pl.Buffered(1)),
                  pl.BlockSpec((None, tn, k), lambda i, j: (idx, j0 + j, 0))],
        out_specs=pl.BlockSpec((tm, tn), lambda i, j: (i, j)),
        out_shape=jax.ShapeDtypeStruct((m, ncols), out_dtype),
        compiler_params=_params("parallel", "arbitrary"),
        name="proj",
    )(x, wt_stack)


def _gate_kernel(h_ref, w_ref, b_ref, o_ref, *, n_gates):
    x = h_ref[...]
    w = w_ref[...]
    x_hi = x.astype(BF16)
    x_lo = (x - x_hi.astype(F32)).astype(BF16)
    w_hi = w.astype(BF16)
    w_lo = (w - w_hi.astype(F32)).astype(BF16)
    acc = _dot_nt(x_hi, w_hi) + _dot_nt(x_lo, w_hi) + _dot_nt(x_hi, w_lo)
    lane = lax.broadcasted_iota(jnp.int32, acc.shape, 1)
    o_ref[...] = jnp.where(lane < n_gates, acc + b_ref[...], 0.0)


def _gate_proj(h32, wt_stack, idx, col0, bias):
    m, d = h32.shape
    g = wt_stack.shape[1] - col0
    assert col0 % LANES == 0 and 0 < g <= LANES
    b_pad = jnp.pad(bias.astype(F32), (0, LANES - g)).reshape(1, LANES)
    tm = _tile(m, GATE_TM)
    return pl.pallas_call(
        functools.partial(_gate_kernel, n_gates=g),
        grid=(m // tm,),
        in_specs=[pl.BlockSpec((tm, d), lambda i: (i, 0)),
                  pl.BlockSpec((None, LANES, d), lambda i: (idx, col0 // LANES, 0)),
                  pl.BlockSpec((1, LANES), lambda i: (0, 0))],
        out_specs=pl.BlockSpec((tm, LANES), lambda i: (i, 0)),
        out_shape=jax.ShapeDtypeStruct((m, LANES), F32),
        compiler_params=_params("parallel"),
        name="gate_proj",
    )(h32, wt_stack, b_pad)


def _mm_res_ln_kernel(a_ref, b_ref, h_ref, g_ref, beta_ref, o32_ref, obf_ref,
                      acc_ref, part_scr, mu_scr, rstd_scr, *, alpha, nk, nh):
    t = pl.program_id(1)
    tm, d = acc_ref.shape
    rows_out = o32_ref.shape[0]
    nch = min(d, LN_NCHUNK)
    chunks = range(0, d, nch)

    @pl.when(t == 0)
    def _():
        a = a_ref[...]
        for n0 in chunks:
            acc_ref[:, n0:n0 + nch] = _dot(a, b_ref[:, n0:n0 + nch])

    @pl.when((t > 0) & (t < nk))
    def _():
        a = a_ref[...]
        for n0 in chunks:
            acc_ref[:, n0:n0 + nch] += _dot(a, b_ref[:, n0:n0 + nch])

    @pl.when(t < nh)
    def _():
        hc = d // nh
        cols = pl.ds(pl.multiple_of(t * hc, hc), hc)
        acc_ref[:, cols] += alpha * h_ref[...]

    @pl.when(t >= nk)
    def _():
        rb = min(rows_out, LN_ROWS)
        row0 = (t - nk) * rows_out
        cols = [slice(c, c + LANES) for c in range(0, d, LANES)]

        def row_groups(fn):
            def body(r, carry):
                local = pl.multiple_of(r * rb, rb)
                fn(pl.ds(local, rb), pl.ds(pl.multiple_of(row0 + local, rb), rb))
                return carry
            lax.fori_loop(0, rows_out // rb, body, 0)

        def lane_mean(part):
            total = jnp.sum(part, axis=1, keepdims=True) * (1.0 / d)
            return jnp.broadcast_to(total, part.shape)

        def tree_sum(xs):
            while len(xs) > 1:
                xs = [xs[n] + xs[n + 1] for n in range(0, len(xs) - 1, 2)] + xs[len(xs) & ~1:]
            return xs[0]

        def pass_sum(rows, src):
            part_scr[rows, :] = tree_sum([acc_ref[src, c] for c in cols])

        def pass_sq(rows, src):
            mu = mu_scr[rows, :]
            dev = [acc_ref[src, c] - mu for c in cols]
            part_scr[rows, :] = tree_sum([x * x for x in dev])

        def pass_norm(rows, src):
            mu = mu_scr[rows, :]
            rstd = rstd_scr[rows, :]
            for c in cols:
                out = (acc_ref[src, c] - mu) * rstd * g_ref[:, c] + beta_ref[:, c]
                o32_ref[rows, c] = out
                obf_ref[rows, c] = out.astype(BF16)

        row_groups(pass_sum)
        mu_scr[...] = lane_mean(part_scr[...])
        row_groups(pass_sq)
        rstd_scr[...] = lax.rsqrt(lane_mean(part_scr[...]) + LN_EPS)
        row_groups(pass_norm)


def _mm_res_ln(a, b, h32, g, beta, alpha):
    m, kdim = a.shape
    d = b.shape[1]
    tm, tk = _tile(m, LN_TM), _tile(kdim, LN_TK)
    nk = kdim // tk
    nh = max(n for n in range(1, min(nk, LN_H_CHUNKS) + 1) if d % (n * LANES) == 0)
    rows_out = min(tm, LN_OUT_ROWS)
    ns = tm // rows_out
    k_of = lambda t: jnp.minimum(t, nk - 1)
    out_row = lambda i, t: (i * ns + jnp.maximum(t - nk, 0), 0)
    return pl.pallas_call(
        functools.partial(_mm_res_ln_kernel, alpha=alpha, nk=nk, nh=nh),
        grid=(m // tm, nk + ns),
        in_specs=[pl.BlockSpec((tm, tk), lambda i, t: (i, k_of(t))),
                  pl.BlockSpec((tk, d), lambda i, t: (k_of(t), 0)),
                  pl.BlockSpec((tm, d // nh), lambda i, t: (i, jnp.minimum(t, nh - 1))),
                  pl.BlockSpec((1, d), lambda i, t: (0, 0)),
                  pl.BlockSpec((1, d), lambda i, t: (0, 0))],
        out_specs=[pl.BlockSpec((rows_out, d), out_row),
                   pl.BlockSpec((rows_out, d), out_row)],
        out_shape=[jax.ShapeDtypeStruct((m, d), F32), jax.ShapeDtypeStruct((m, d), BF16)],
        scratch_shapes=[pltpu.VMEM((tm, d), F32)] + [pltpu.VMEM((rows_out, LANES), F32)] * 3,
        compiler_params=_params("parallel", "arbitrary"),
        name="proj_res_ln",
    )(a, b, h32, g.reshape(1, d).astype(F32), beta.reshape(1, d).astype(F32))


def _ffn_up_kernel(x_ref, wg_ref, wu_ref, o_ref, *, n_real):
    j = pl.program_id(1)

    @pl.when(j < n_real)
    def _():
        x = x_ref[...]
        gate = _dot(x, wg_ref[...].astype(BF16))
        up = _dot(x, wu_ref[...].astype(BF16))
        o_ref[...] = (gate * _sigmoid(gate) * up).astype(o_ref.dtype)

    @pl.when(j >= n_real)
    def _():
        o_ref[...] = jnp.zeros_like(o_ref)


def _ffn_up(x, w_up_stack, idx, ffn_pad):
    m, d = x.shape
    f = w_up_stack.shape[2] // 2
    tm, tn = _tile(m, BIG_TM), _tile(f, FFN_TN)
    assert ffn_pad % tn == 0
    n_real = f // tn
    real = lambda j: jnp.minimum(j, n_real - 1)
    return pl.pallas_call(
        functools.partial(_ffn_up_kernel, n_real=n_real),
        grid=(m // tm, ffn_pad // tn),
        in_specs=[pl.BlockSpec((tm, d), lambda i, j: (i, 0), pipeline_mode=pl.Buffered(1)),
                  pl.BlockSpec((None, d, tn), lambda i, j: (idx, 0, real(j))),
                  pl.BlockSpec((None, d, tn), lambda i, j: (idx, 0, n_real + real(j)))],
        out_specs=pl.BlockSpec((tm, tn), lambda i, j: (i, j)),
        out_shape=jax.ShapeDtypeStruct((m, ffn_pad), BF16),
        compiler_params=_params("parallel", "arbitrary"),
        name="ffn_up",
    )(x, w_up_stack, w_up_stack)


def _mlstm_kernel(q_ref, k_ref, v_ref, o_ref, gc_ref, gri_ref, grf_ref, ng_ref, out_ref,
                  ct_ref, m_ref, *, chunk, dk, dv, heads, hps):
    hp = pl.program_id(1)
    c = pl.program_id(2)
    L = chunk

    @pl.when(c == 0)
    def _():
        ct_ref[...] = jnp.zeros_like(ct_ref)
        m_ref[...] = jnp.zeros_like(m_ref)

    gc = gc_ref[...]
    lane = lax.broadcasted_iota(jnp.int32, (L, LANES), 1)
    ti = lax.broadcasted_iota(jnp.int32, (L, L), 0)
    si = lax.broadcasted_iota(jnp.int32, (L, L), 1)
    causal = si <= ti
    tri_lo = causal.astype(BF16)
    tri_up = (ti <= si).astype(BF16)
    one_lane = (lane == 0).astype(F32)

    for hl in range(hps):
        h = hp * hps + hl
        kcols = slice(hl * dk, (hl + 1) * dk)
        vcols = slice(hl * dv, (hl + 1) * dv)

        ig_c = jnp.sum(jnp.where(lane == h, gc, 0.0), axis=1, keepdims=True)
        fg_c = jnp.sum(jnp.where(lane == heads + h, gc, 0.0), axis=1, keepdims=True)
        ig_r = gri_ref[pl.ds(h, 1), :]
        fg_r = grf_ref[pl.ds(h, 1), :]
        ig_c, ig_r = _soft_cap(ig_c), _soft_cap(ig_r)
        lf_c, lf_r = _log_sigmoid(_soft_cap(fg_c)), _log_sigmoid(_soft_cap(fg_r))

        b_c = sum(_dot(tri_lo, p) for p in _split3(jnp.broadcast_to(lf_c, (L, LANES))))[:, :1]
        b_r = sum(_dot(p, tri_up) for p in _split3(jnp.broadcast_to(lf_r, (SUBLANES, L))))[:1, :]

        m_prev = m_ref[hl]
        d_log = jnp.where(causal, b_c - b_r + ig_r, NEG_BIG)
        g_log = b_c + m_prev
        m_t = jnp.maximum(g_log, jnp.max(d_log, axis=1, keepdims=True))
        w_intra = jnp.exp(d_log - m_t)
        w_inter = jnp.exp(g_log - m_t)

        q = q_ref[:, kcols]
        k = k_ref[:, kcols]
        v = v_ref[:, vcols]
        s = _dot_nt(q, k) * w_intra
        ct = ct_ref[hl]
        inter = _dot(q, ct.astype(BF16))
        num = _dot(s.astype(BF16), v) + w_inter * inter[:, :dv]
        den = jnp.sum(s, axis=1, keepdims=True) + w_inter * inter[:, dv:dv + 1]
        hh = num * (1.0 / jnp.maximum(jnp.abs(den), jnp.exp(-m_t)))

        b_end = b_c[L - 1:L, :]
        a_c = b_end - b_c + ig_c
        a_r = b_end - b_r + ig_r
        m_new = jnp.maximum(b_end + m_prev, jnp.max(a_r, axis=1, keepdims=True))
        decay = jnp.exp(b_end + m_prev - m_new)
        wa_c = jnp.exp(a_c - m_new)
        wv = jnp.concatenate([wa_c * v.astype(F32), wa_c * one_lane], axis=1).astype(BF16)
        ct_ref[hl] = decay * ct + _dot_tn(k, wv)
        m_ref[hl] = m_new

        hn = hh * lax.rsqrt(jnp.mean(hh * hh, axis=1, keepdims=True) + HEAD_NORM_EPS)
        out_ref[:, vcols] = (hn * ng_ref[:, vcols] * _sigmoid(o_ref[:, vcols])).astype(out_ref.dtype)


def _mlstm_core(qkv, o, gates, gates_t, norm_g, batch, seq):
    heads = MLSTM_HEADS
    assert heads == SUBLANES, "row-form gate blocks assume one sublane tile per gate kind"
    m = qkv.shape[0]
    hv = o.shape[1]
    dv = hv // heads
    dk = (qkv.shape[1] - hv) // (2 * heads)
    assert 2 * heads * dk == hv, "v blocks are addressed in units of dv"
    hps = MLSTM_HEADS_PER_STEP
    assert heads % hps == 0
    groups = heads // hps
    chunk = _tile(seq, MLSTM_CHUNK)
    nc = seq // chunk
    row = lambda b, h, c: b * nc + c
    return pl.pallas_call(
        functools.partial(_mlstm_kernel, chunk=chunk, dk=dk, dv=dv, heads=heads, hps=hps),
        grid=(batch, groups, nc),
        in_specs=[pl.BlockSpec((chunk, hps * dk), lambda b, h, c: (row(b, h, c), h)),
                  pl.BlockSpec((chunk, hps * dk), lambda b, h, c: (row(b, h, c), groups + h)),
                  pl.BlockSpec((chunk, hps * dv), lambda b, h, c: (row(b, h, c), groups + h)),
                  pl.BlockSpec((chunk, hps * dv), lambda b, h, c: (row(b, h, c), h)),
                  pl.BlockSpec((chunk, LANES), lambda b, h, c: (row(b, h, c), 0)),
                  pl.BlockSpec((SUBLANES, chunk), lambda b, h, c: (0, row(b, h, c))),
                  pl.BlockSpec((SUBLANES, chunk), lambda b, h, c: (1, row(b, h, c))),
                  pl.BlockSpec((1, hps * dv), lambda b, h, c: (0, h))],
        out_specs=pl.BlockSpec((chunk, hps * dv), lambda b, h, c: (row(b, h, c), h)),
        out_shape=jax.ShapeDtypeStruct((m, hv), BF16),
        scratch_shapes=[pltpu.VMEM((hps, dk, dv + LANES), F32), pltpu.VMEM((hps, 1, 1), F32)],
        compiler_params=_params("parallel", "parallel", "arbitrary"),
        name="mlstm_core",
    )(qkv, qkv, qkv, o, gates, gates_t, gates_t, norm_g.reshape(1, hv).astype(F32))


def _fox_cum_kernel(g_ref, cum_ref, carry_ref, *, ts):
    @pl.when(pl.program_id(1) == 0)
    def _():
        carry_ref[...] = jnp.zeros_like(carry_ref)

    logf = _log_sigmoid(g_ref[...])
    ti = lax.broadcasted_iota(jnp.int32, (ts, ts), 0)
    si = lax.broadcasted_iota(jnp.int32, (ts, ts), 1)
    tri_lo = (si <= ti).astype(BF16)
    cum = sum(_dot(tri_lo, p) for p in _split3(logf)) + carry_ref[...]
    carry_ref[...] = cum[ts - 1:ts, :]
    cum_ref[...] = cum


def _fox_cumsum(gates, batch, seq):
    ts = _tile(seq, FOX_TS)
    ns = seq // ts
    return pl.pallas_call(
        functools.partial(_fox_cum_kernel, ts=ts),
        grid=(batch, ns),
        in_specs=[pl.BlockSpec((ts, LANES), lambda b, s: (b * ns + s, 0))],
        out_specs=pl.BlockSpec((ts, LANES), lambda b, s: (b * ns + s, 0)),
        out_shape=jax.ShapeDtypeStruct((batch * seq, LANES), F32),
        scratch_shapes=[pltpu.VMEM((1, LANES), F32)],
        compiler_params=_params("parallel", "arbitrary"),
        name="fox_cumsum",
    )(gates)


def _bias_lanes(col, own_first):
    n = col.shape[0]
    hi, mid, lo = (p.astype(F32) for p in _split3(col))
    lane = lax.broadcasted_iota(jnp.int32, (n, LANES), 1)
    base = 0 if own_first else 3
    ones = ((lane >= 3 - base) & (lane < 6 - base)).astype(F32)
    out = jnp.where(lane == base, hi, jnp.where(lane == base + 1, mid, jnp.where(lane == base + 2, lo, ones)))
    return out.astype(BF16)


def _fox_attn_kernel(q_ref, k_ref, v_ref, cc_ref, o_ref, ka_scr, vt_scr, s_scr, p_scr,
                     a_scr, m_scr, l_scr, acc_scr, *, tq, seq, hps):
    hp = pl.program_id(1)
    i = pl.program_id(2)
    dh = LANES
    local_heads = range(hps)

    def head_col(rows, hl):
        lane = lax.broadcasted_iota(jnp.int32, rows.shape, 1)
        return jnp.sum(jnp.where(lane == hp * hps + hl, rows, 0.0), axis=1, keepdims=True) * LOG2E

    @pl.when(i == 0)
    def _():
        for hl in local_heads:
            hc = slice(hl * dh, (hl + 1) * dh)
            for r0 in range(0, seq, tq):
                ka_scr[hl, r0:r0 + tq, :LANES] = k_ref[r0:r0 + tq, hc]
                ka_scr[hl, r0:r0 + tq, LANES:] = _bias_lanes(-head_col(cc_ref[r0:r0 + tq, :], hl), True)
                vt_scr[hl, :, r0:r0 + tq] = v_ref[r0:r0 + tq, hc].astype(F32).T.astype(BF16)

    q0 = pl.multiple_of(i * tq, tq)
    cum_q = cc_ref[pl.ds(q0, tq), :]
    qa = [jnp.concatenate([q_ref[:, hl * dh:(hl + 1) * dh], _bias_lanes(head_col(cum_q, hl), False)], axis=1)
          for hl in local_heads]

    m_scr[...] = jnp.full_like(m_scr, NEG_BIG)
    l_scr[...] = jnp.zeros_like(l_scr)
    acc_scr[...] = jnp.zeros_like(acc_scr)
    a_scr[...] = jnp.ones_like(a_scr)
    for hl in local_heads:
        p_scr[hl, 1] = jnp.zeros((tq, tq), BF16)

    def logits(j, parity):
        start = pl.multiple_of(j * tq, tq)
        for hl in local_heads:
            s_scr[hl, parity] = _dot_nt(ka_scr[hl, pl.ds(start, tq), :], qa[hl])

    def softmax(parity, on_diagonal):
        for hl in local_heads:
            st = s_scr[hl, parity]
            if on_diagonal:
                ki = lax.broadcasted_iota(jnp.int32, (tq, tq), 0)
                qi = lax.broadcasted_iota(jnp.int32, (tq, tq), 1)
                st = jnp.where(ki <= qi, st, NEG_BIG)
            m_old = m_scr[hl]
            m_new = jnp.maximum(m_old, jnp.max(st, axis=0, keepdims=True))
            p = jnp.exp2(st - m_new)
            a = jnp.exp2(m_old - m_new)
            l_scr[hl] = a * l_scr[hl] + jnp.sum(p, axis=0, keepdims=True)
            m_scr[hl] = m_new
            p_scr[hl, parity] = p.astype(BF16)
            a_scr[hl] = a

    def values(j, parity):
        start = pl.multiple_of(jnp.maximum(j, 0) * tq, tq)
        for hl in local_heads:
            acc_scr[hl] = a_scr[hl] * acc_scr[hl] + _dot(vt_scr[hl, :, pl.ds(start, tq)], p_scr[hl, parity])

    def step(j, parity):
        values(j - 1, 1 - parity)
        logits(j + 1, 1 - parity)
        softmax(parity, False)

    def finish(parity):
        values(i - 1, 1 - parity)
        softmax(parity, True)
        values(i, parity)

    logits(0, 0)

    def pair(t, carry):
        step(2 * t, 0)
        step(2 * t + 1, 1)
        return carry

    lax.fori_loop(0, i // 2, pair, 0)

    @pl.when(i % 2 == 0)
    def _():
        finish(0)

    @pl.when(i % 2 == 1)
    def _():
        step(i - 1, 0)
        finish(1)

    for hl in local_heads:
        o_ref[:, hl * dh:(hl + 1) * dh] = (acc_scr[hl] * (1.0 / l_scr[hl])).T.astype(o_ref.dtype)


def _fox_attn(qkv, cum, batch, seq):
    dh = FOX_HEAD_DIM
    assert dh == LANES
    m = qkv.shape[0]
    heads = qkv.shape[1] // (3 * dh)
    assert heads <= LANES
    hps = FOX_HEADS_PER_STEP if heads % FOX_HEADS_PER_STEP == 0 else 1
    groups = heads // hps
    tq = _tile(seq, FOX_TQ)
    nq = seq // tq
    return pl.pallas_call(
        functools.partial(_fox_attn_kernel, tq=tq, seq=seq, hps=hps),
        grid=(batch, groups, nq),
        in_specs=[pl.BlockSpec((tq, hps * dh), lambda b, h, i: (b * nq + i, h)),
                  pl.BlockSpec((seq, hps * dh), lambda b, h, i: (b, groups + h)),
                  pl.BlockSpec((seq, hps * dh), lambda b, h, i: (b, 2 * groups + h)),
                  pl.BlockSpec((seq, LANES), lambda b, h, i: (b, 0))],
        out_specs=pl.BlockSpec((tq, hps * dh), lambda b, h, i: (b * nq + i, h)),
        out_shape=jax.ShapeDtypeStruct((m, heads * dh), BF16),
        scratch_shapes=[pltpu.VMEM((hps, seq, 2 * LANES), BF16), pltpu.VMEM((hps, dh, seq), BF16),
                        pltpu.VMEM((hps, 2, tq, tq), F32), pltpu.VMEM((hps, 2, tq, tq), BF16),
                        pltpu.VMEM((hps, 1, tq), F32), pltpu.VMEM((hps, 1, tq), F32),
                        pltpu.VMEM((hps, 1, tq), F32), pltpu.VMEM((hps, dh, tq), F32)],
        compiler_params=_params("parallel", "parallel", "arbitrary"),
        name="fox_attn",
    )(qkv, qkv, qkv, cum)


def kernel(x, mlstm_w_in, mlstm_b_gate, mlstm_norm_g, mlstm_w_out, fox_w_in, fox_b_f, fox_w_out,
           ln_mix_g, ln_mix_b, ffn_w_up, ffn_w_down, ln_ffn_g, ln_ffn_b):
    batch, seq, d = x.shape
    depth = ln_mix_g.shape[0]
    alpha = (2 * depth) ** 0.25
    m = batch * seq
    ffn = ffn_w_down.shape[1]
    ffn_pad = -(-ffn // FFN_PAD) * FFN_PAD
    hv = mlstm_w_out.shape[1]
    mlstm_main = mlstm_w_in.shape[2] - 2 * MLSTM_HEADS
    mlstm_qk = (mlstm_main - 2 * hv) // 2
    mlstm_dk = mlstm_qk // MLSTM_HEADS
    fox_w = fox_w_out.shape[1]

    mlstm_wt = jnp.swapaxes(mlstm_w_in, 1, 2)
    fox_wt = jnp.swapaxes(fox_w_in, 1, 2)

    h32 = x.reshape(m, d)
    hbf = h32.astype(BF16)
    for layer in range(depth):
        slot = layer // N_MIXERS
        if layer % N_MIXERS == 0:
            qkv = _proj(hbf, mlstm_wt, slot, 0, mlstm_main - hv, BF16, mlstm_qk, mlstm_dk ** -0.5)
            o = _proj(hbf, mlstm_wt, slot, mlstm_main - hv, hv, F32)
            gates = _gate_proj(h32, mlstm_wt, slot, mlstm_main, mlstm_b_gate[slot])
            mixed = _mlstm_core(qkv, o, gates, gates.T, mlstm_norm_g[slot], batch, seq)
            w_out = _cast_pad_rows(mlstm_w_out, slot, hv)
        else:
            qkv = _proj(hbf, fox_wt, slot, 0, 3 * fox_w, BF16, fox_w, FOX_HEAD_DIM ** -0.5 * LOG2E)
            gates = _gate_proj(h32, fox_wt, slot, 3 * fox_w, fox_b_f[slot])
            mixed = _fox_attn(qkv, _fox_cumsum(gates, batch, seq), batch, seq)
            w_out = _cast_pad_rows(fox_w_out, slot, fox_w)
        h32, hbf = _mm_res_ln(mixed, w_out, h32, ln_mix_g[layer], ln_mix_b[layer], alpha)

        hidden = _ffn_up(hbf, ffn_w_up, layer, ffn_pad)
        w_down = _cast_pad_rows(ffn_w_down, layer, ffn_pad)
        h32, hbf = _mm_res_ln(hidden, w_down, h32, ln_ffn_g[layer], ln_ffn_b[layer], alpha)
    return h32.reshape(batch, seq, d)
```

```python
import functools
import math

import jax
import jax.numpy as jnp
from jax import lax
from jax.experimental import pallas as pl
from jax.experimental.pallas import tpu as pltpu

F32 = jnp.float32
BF16 = jnp.bfloat16

MLSTM_HEADS = 8
FOX_HEAD_DIM = 128
GATE_SOFT_CAP = 15.0
HEAD_NORM_EPS = 1e-6
LN_EPS = 1e-5
N_MIXERS = 2

LANES = 128
SUBLANES = 8
NEG_BIG = -1e30
LOG2E = math.log2(math.e)

BIG_TM = 2048
MM_TN = 512
FFN_TN = 256
LN_TM = 1024
LN_OUT_ROWS = 128
LN_TK = 1024
LN_H_CHUNKS = 8
LN_NCHUNK = 1024
LN_ROWS = 16
GATE_TM = 512
CAST_ROWS = 256
FFN_PAD = 1024
MLSTM_CHUNK = 256
MLSTM_HEADS_PER_STEP = 8
FOX_TQ = 512
FOX_TS = 512
FOX_HEADS_PER_STEP = 4
VMEM_LIMIT = 56 * 1024 * 1024


def _tile(n, pref):
    if n <= pref:
        return n
    t = (pref // LANES) * LANES
    while t >= LANES:
        if n % t == 0:
            return t
        t -= LANES
    raise ValueError(f"no lane-aligned tile of {n} below {pref}")


def _params(*sem):
    return pltpu.CompilerParams(dimension_semantics=sem, vmem_limit_bytes=VMEM_LIMIT)


def _dot(a, b):
    return jnp.dot(a, b, preferred_element_type=F32)


def _dot_nt(a, b):
    return lax.dot_general(a, b, (((1,), (1,)), ((), ())), preferred_element_type=F32)


def _dot_tn(a, b):
    return lax.dot_general(a, b, (((0,), (0,)), ((), ())), preferred_element_type=F32)


def _split3(x):
    hi = x.astype(BF16)
    r1 = x - hi.astype(F32)
    mid = r1.astype(BF16)
    lo = (r1 - mid.astype(F32)).astype(BF16)
    return hi, mid, lo


def _sigmoid(x):
    return 1.0 / (1.0 + jnp.exp(-x))


def _log_sigmoid(x):
    return jnp.minimum(x, 0.0) - jnp.log1p(jnp.exp(-jnp.abs(x)))


def _soft_cap(z):
    return GATE_SOFT_CAP * jnp.tanh(z / GATE_SOFT_CAP)


def _cast_kernel(w_ref, o_ref, *, n_real):
    r = pl.program_id(0)

    @pl.when(r < n_real)
    def _():
        o_ref[...] = w_ref[...].astype(BF16)

    @pl.when(r >= n_real)
    def _():
        o_ref[...] = jnp.zeros_like(o_ref)


def _cast_pad_rows(w_stack, idx, rows_pad):
    _, rows, cols = w_stack.shape
    tr = _tile(rows, CAST_ROWS)
    assert rows_pad % tr == 0
    n_real = rows // tr
    return pl.pallas_call(
        functools.partial(_cast_kernel, n_real=n_real),
        grid=(rows_pad // tr,),
        in_specs=[pl.BlockSpec((None, tr, cols), lambda r: (idx, jnp.minimum(r, n_real - 1), 0))],
        out_specs=pl.BlockSpec((tr, cols), lambda r: (r, 0)),
        out_shape=jax.ShapeDtypeStruct((rows_pad, cols), BF16),
        compiler_params=_params("parallel"),
        name="cast_w",
    )(w_stack)


def _proj_kernel(x_ref, w_ref, o_ref, *, n_scaled, scale):
    j = pl.program_id(1)
    acc = _dot_nt(x_ref[...], w_ref[...].astype(BF16))
    if n_scaled:
        acc = acc * jnp.where(j < n_scaled, scale, 1.0)
    o_ref[...] = acc.astype(o_ref.dtype)


def _proj(x, wt_stack, idx, col0, ncols, out_dtype, scaled_cols=0, scale=1.0):
    m, k = x.shape
    tm, tn = _tile(m, BIG_TM), _tile(ncols, MM_TN)
    assert col0 % tn == 0 and scaled_cols % tn == 0
    j0 = col0 // tn
    return pl.pallas_call(
        functools.partial(_proj_kernel, n_scaled=scaled_cols // tn, scale=scale),
        grid=(m // tm, ncols // tn),
        in_specs=[pl.BlockSpec((tm, k), lambda i, j: (i, 0), pipeline_mode=pl.Buffered(1)),
                  pl.BlockSpec((None, tn, k), lambda i, j: (idx, j0 + j, 0))],
        out_specs=pl.BlockSpec((tm, tn), lambda i, j: (i, j)),
        out_shape=jax.ShapeDtypeStruct((m, ncols), out_dtype),
        compiler_params=_params("parallel", "arbitrary"),
        name="proj",
    )(x, wt_stack)


def _gate_kernel(h_ref, w_ref, b_ref, o_ref, *, n_gates):
    x = h_ref[...]
    w = w_ref[...]
    x_hi = x.astype(BF16)
    x_lo = (x - x_hi.astype(F32)).astype(BF16)
    w_hi = w.astype(BF16)
    w_lo = (w - w_hi.astype(F32)).astype(BF16)
    acc = _dot_nt(x_hi, w_hi) + _dot_nt(x_lo, w_hi) + _dot_nt(x_hi, w_lo)
    lane = lax.broadcasted_iota(jnp.int32, acc.shape, 1)
    o_ref[...] = jnp.where(lane < n_gates, acc + b_ref[...], 0.0)


def _gate_proj(h32, wt_stack, idx, col0, bias):
    m, d = h32.shape
    g = wt_stack.shape[1] - col0
    assert col0 % LANES == 0 and 0 < g <= LANES
    b_pad = jnp.pad(bias.astype(F32), (0, LANES - g)).reshape(1, LANES)
    tm = _tile(m, GATE_TM)
    return pl.pallas_call(
        functools.partial(_gate_kernel, n_gates=g),
        grid=(m // tm,),
        in_specs=[pl.BlockSpec((tm, d), lambda i: (i, 0)),
                  pl.BlockSpec((None, LANES, d), lambda i: (idx, col0 // LANES, 0)),
                  pl.BlockSpec((1, LANES), lambda i: (0, 0))],
        out_specs=pl.BlockSpec((tm, LANES), lambda i: (i, 0)),
        out_shape=jax.ShapeDtypeStruct((m, LANES), F32),
        compiler_params=_params("parallel"),
        name="gate_proj",
    )(h32, wt_stack, b_pad)


def _mm_res_ln_kernel(a_ref, b_ref, h_ref, g_ref, beta_ref, o32_ref, obf_ref,
                      acc_ref, part_scr, mu_scr, rstd_scr, *, alpha, nk, nh):
    t = pl.program_id(1)
    tm, d = acc_ref.shape
    rows_out = o32_ref.shape[0]
    nch = min(d, LN_NCHUNK)
    chunks = range(0, d, nch)

    @pl.when(t == 0)
    def _():
        a = a_ref[...]
        for n0 in chunks:
            acc_ref[:, n0:n0 + nch] = _dot(a, b_ref[:, n0:n0 + nch])

    @pl.when((t > 0) & (t < nk))
    def _():
        a = a_ref[...]
        for n0 in chunks:
            acc_ref[:, n0:n0 + nch] += _dot(a, b_ref[:, n0:n0 + nch])

    @pl.when(t < nh)
    def _():
        hc = d // nh
        cols = pl.ds(pl.multiple_of(t * hc, hc), hc)
        acc_ref[:, cols] += alpha * h_ref[...]

    @pl.when(t >= nk)
    def _():
        rb = min(rows_out, LN_ROWS)
        row0 = (t - nk) * rows_out
        cols = [slice(c, c + LANES) for c in range(0, d, LANES)]

        def row_groups(fn):
            def body(r, carry):
                local = pl.multiple_of(r * rb, rb)
                fn(pl.ds(local, rb), pl.ds(pl.multiple_of(row0 + local, rb), rb))
                return carry
            lax.fori_loop(0, rows_out // rb, body, 0)

        def lane_mean(part):
            total = jnp.sum(part, axis=1, keepdims=True) * (1.0 / d)
            return jnp.broadcast_to(total, part.shape)

        def tree_sum(xs):
            while len(xs) > 1:
                xs = [xs[n] + xs[n + 1] for n in range(0, len(xs) - 1, 2)] + xs[len(xs) & ~1:]
            return xs[0]

        def pass_sum(rows, src):
            part_scr[rows, :] = tree_sum([acc_ref[src, c] for c in cols])

        def pass_sq(rows, src):
            mu = mu_scr[rows, :]
            dev = [acc_ref[src, c] - mu for c in cols]
            part_scr[rows, :] = tree_sum([x * x for x in dev])

        def pass_norm(rows, src):
            mu = mu_scr[rows, :]
            rstd = rstd_scr[rows, :]
            for c in cols:
                out = (acc_ref[src, c] - mu) * rstd * g_ref[:, c] + beta_ref[:, c]
                o32_ref[rows, c] = out
                obf_ref[rows, c] = out.astype(BF16)

        row_groups(pass_sum)
        mu_scr[...] = lane_mean(part_scr[...])
        row_groups(pass_sq)
        rstd_scr[...] = lax.rsqrt(lane_mean(part_scr[...]) + LN_EPS)
        row_groups(pass_norm)


def _mm_res_ln(a, b, h32, g, beta, alpha):
    m, kdim = a.shape
    d = b.shape[1]
    tm, tk = _tile(m, LN_TM), _tile(kdim, LN_TK)
    nk = kdim // tk
    nh = max(n for n in range(1, min(nk, LN_H_CHUNKS) + 1) if d % (n * LANES) == 0)
    rows_out = min(tm, LN_OUT_ROWS)
    ns = tm // rows_out
    ni = m // tm
    sweep = lambda t: t >= nk
    i_of = lambda i, t: jnp.where(sweep(t), jnp.minimum(i + 1, ni - 1), i)
    k_of = lambda t: jnp.where(sweep(t), 0, t)
    out_row = lambda i, t: (i * ns + jnp.maximum(t - nk, 0), 0)
    return pl.pallas_call(
        functools.partial(_mm_res_ln_kernel, alpha=alpha, nk=nk, nh=nh),
        grid=(ni, nk + ns),
        in_specs=[pl.BlockSpec((tm, tk), lambda i, t: (i_of(i, t), k_of(t))),
                  pl.BlockSpec((tk, d), lambda i, t: (k_of(t), 0)),
                  pl.BlockSpec((tm, d // nh), lambda i, t: (i_of(i, t), jnp.minimum(k_of(t), nh - 1))),
                  pl.BlockSpec((1, d), lambda i, t: (0, 0)),
                  pl.BlockSpec((1, d), lambda i, t: (0, 0))],
        out_specs=[pl.BlockSpec((rows_out, d), out_row),
                   pl.BlockSpec((rows_out, d), out_row)],
        out_shape=[jax.ShapeDtypeStruct((m, d), F32), jax.ShapeDtypeStruct((m, d), BF16)],
        scratch_shapes=[pltpu.VMEM((tm, d), F32)] + [pltpu.VMEM((rows_out, LANES), F32)] * 3,
        compiler_params=_params("parallel", "arbitrary"),
        name="proj_res_ln",
    )(a, b, h32, g.reshape(1, d).astype(F32), beta.reshape(1, d).astype(F32))


def _ffn_up_kernel(x_ref, wg_ref, wu_ref, o_ref, *, n_real):
    j = pl.program_id(1)

    @pl.when(j < n_real)
    def _():
        x = x_ref[...]
        gate = _dot(x, wg_ref[...].astype(BF16))
        up = _dot(x, wu_ref[...].astype(BF16))
        o_ref[...] = (gate * _sigmoid(gate) * up).astype(o_ref.dtype)

    @pl.when(j >= n_real)
    def _():
        o_ref[...] = jnp.zeros_like(o_ref)


def _ffn_up(x, w_up_stack, idx, ffn_pad):
    m, d = x.shape
    f = w_up_stack.shape[2] // 2
    tm, tn = _tile(m, BIG_TM), _tile(f, FFN_TN)
    assert ffn_pad % tn == 0
    n_real = f // tn
    real = lambda j: jnp.minimum(j, n_real - 1)
    return pl.pallas_call(
        functools.partial(_ffn_up_kernel, n_real=n_real),
        grid=(m // tm, ffn_pad // tn),
        in_specs=[pl.BlockSpec((tm, d), lambda i, j: (i, 0), pipeline_mode=pl.Buffered(1)),
                  pl.BlockSpec((None, d, tn), lambda i, j: (idx, 0, real(j))),
                  pl.BlockSpec((None, d, tn), lambda i, j: (idx, 0, n_real + real(j)))],
        out_specs=pl.BlockSpec((tm, tn), lambda i, j: (i, j)),
        out_shape=jax.ShapeDtypeStruct((m, ffn_pad), BF16),
        compiler_params=_params("parallel", "arbitrary"),
        name="ffn_up",
    )(x, w_up_stack, w_up_stack)


def _mlstm_kernel(q_ref, k_ref, v_ref, o_ref, gc_ref, gri_ref, grf_ref, ng_ref, out_ref,
                  ct_ref, m_ref, *, chunk, dk, dv, heads, hps):
    hp = pl.program_id(1)
    c = pl.program_id(2)
    L = chunk

    @pl.when(c == 0)
    def _():
        ct_ref[...] = jnp.zeros_like(ct_ref)
        m_ref[...] = jnp.zeros_like(m_ref)

    gc = gc_ref[...]
    lane = lax.broadcasted_iota(jnp.int32, (L, LANES), 1)
    ti = lax.broadcasted_iota(jnp.int32, (L, L), 0)
    si = lax.broadcasted_iota(jnp.int32, (L, L), 1)
    causal = si <= ti
    tri_lo = causal.astype(BF16)
    tri_up = (ti <= si).astype(BF16)
    one_lane = (lane == 0).astype(F32)

    for hl in range(hps):
        h = hp * hps + hl
        kcols = slice(hl * dk, (hl + 1) * dk)
        vcols = slice(hl * dv, (hl + 1) * dv)

        ig_c = jnp.sum(jnp.where(lane == h, gc, 0.0), axis=1, keepdims=True)
        fg_c = jnp.sum(jnp.where(lane == heads + h, gc, 0.0), axis=1, keepdims=True)
        ig_r = gri_ref[pl.ds(h, 1), :]
        fg_r = grf_ref[pl.ds(h, 1), :]
        ig_c, ig_r = _soft_cap(ig_c), _soft_cap(ig_r)
        lf_c, lf_r = _log_sigmoid(_soft_cap(fg_c)), _log_sigmoid(_soft_cap(fg_r))

        b_c = sum(_dot(tri_lo, p) for p in _split3(jnp.broadcast_to(lf_c, (L, LANES))))[:, :1]
        b_r = sum(_dot(p, tri_up) for p in _split3(jnp.broadcast_to(lf_r, (SUBLANES, L))))[:1, :]

        m_prev = m_ref[hl]
        d_log = jnp.where(causal, b_c - b_r + ig_r, NEG_BIG)
        g_log = b_c + m_prev
        m_t = jnp.maximum(g_log, jnp.max(d_log, axis=1, keepdims=True))
        w_intra = jnp.exp(d_log - m_t)
        w_inter = jnp.exp(g_log - m_t)

        q = q_ref[:, kcols]
        k = k_ref[:, kcols]
        v = v_ref[:, vcols]
        s = _dot_nt(q, k) * w_intra
        ct = ct_ref[hl]
        inter = _dot(q, ct.astype(BF16))
        num = _dot(s.astype(BF16), v) + w_inter * inter[:, :dv]
        den = jnp.sum(s, axis=1, keepdims=True) + w_inter * inter[:, dv:dv + 1]
        hh = num * (1.0 / jnp.maximum(jnp.abs(den), jnp.exp(-m_t)))

        b_end = b_c[L - 1:L, :]
        a_c = b_end - b_c + ig_c
        a_r = b_end - b_r + ig_r
        m_new = jnp.maximum(b_end + m_prev, jnp.max(a_r, axis=1, keepdims=True))
        decay = jnp.exp(b_end + m_prev - m_new)
        wa_c = jnp.exp(a_c - m_new)
        wv = jnp.concatenate([wa_c * v.astype(F32), wa_c * one_lane], axis=1).astype(BF16)
        ct_ref[hl] = decay * ct + _dot_tn(k, wv)
        m_ref[hl] = m_new

        hn = hh * lax.rsqrt(jnp.mean(hh * hh, axis=1, keepdims=True) + HEAD_NORM_EPS)
        out_ref[:, vcols] = (hn * ng_ref[:, vcols] * _sigmoid(o_ref[:, vcols])).astype(out_ref.dtype)


def _mlstm_core(qkv, o, gates, gates_t, norm_g, batch, seq):
    heads = MLSTM_HEADS
    assert heads == SUBLANES, "row-form gate blocks assume one sublane tile per gate kind"
    m = qkv.shape[0]
    hv = o.shape[1]
    dv = hv // heads
    dk = (qkv.shape[1] - hv) // (2 * heads)
    assert 2 * heads * dk == hv, "v blocks are addressed in units of dv"
    hps = MLSTM_HEADS_PER_STEP
    assert heads % hps == 0
    groups = heads // hps
    chunk = _tile(seq, MLSTM_CHUNK)
    nc = seq // chunk
    row = lambda b, h, c: b * nc + c
    return pl.pallas_call(
        functools.partial(_mlstm_kernel, chunk=chunk, dk=dk, dv=dv, heads=heads, hps=hps),
        grid=(batch, groups, nc),
        in_specs=[pl.BlockSpec((chunk, hps * dk), lambda b, h, c: (row(b, h, c), h)),
                  pl.BlockSpec((chunk, hps * dk), lambda b, h, c: (row(b, h, c), groups + h)),
                  pl.BlockSpec((chunk, hps * dv), lambda b, h, c: (row(b, h, c), groups + h)),
                  pl.BlockSpec((chunk, hps * dv), lambda b, h, c: (row(b, h, c), h)),
                  pl.BlockSpec((chunk, LANES), lambda b, h, c: (row(b, h, c), 0)),
                  pl.BlockSpec((SUBLANES, chunk), lambda b, h, c: (0, row(b, h, c))),
                  pl.BlockSpec((SUBLANES, chunk), lambda b, h, c: (1, row(b, h, c))),
                  pl.BlockSpec((1, hps * dv), lambda b, h, c: (0, h))],
        out_specs=pl.BlockSpec((chunk, hps * dv), lambda b, h, c: (row(b, h, c), h)),
        out_shape=jax.ShapeDtypeStruct((m, hv), BF16),
        scratch_shapes=[pltpu.VMEM((hps, dk, dv + LANES), F32), pltpu.VMEM((hps, 1, 1), F32)],
        compiler_params=_params("parallel", "parallel", "arbitrary"),
        name="mlstm_core",
    )(qkv, qkv, qkv, o, gates, gates_t, gates_t, norm_g.reshape(1, hv).astype(F32))


def _fox_cum_kernel(g_ref, cum_ref, carry_ref, *, ts):
    @pl.when(pl.program_id(1) == 0)
    def _():
        carry_ref[...] = jnp.zeros_like(carry_ref)

    logf = _log_sigmoid(g_ref[...])
    ti = lax.broadcasted_iota(jnp.int32, (ts, ts), 0)
    si = lax.broadcasted_iota(jnp.int32, (ts, ts), 1)
    tri_lo = (si <= ti).astype(BF16)
    cum = sum(_dot(tri_lo, p) for p in _split3(logf)) + carry_ref[...]
    carry_ref[...] = cum[ts - 1:ts, :]
    cum_ref[...] = cum


def _fox_cumsum(gates, batch, seq):
    ts = _tile(seq, FOX_TS)
    ns = seq // ts
    return pl.pallas_call(
        functools.partial(_fox_cum_kernel, ts=ts),
        grid=(batch, ns),
        in_specs=[pl.BlockSpec((ts, LANES), lambda b, s: (b * ns + s, 0))],
        out_specs=pl.BlockSpec((ts, LANES), lambda b, s: (b * ns + s, 0)),
        out_shape=jax.ShapeDtypeStruct((batch * seq, LANES), F32),
        scratch_shapes=[pltpu.VMEM((1, LANES), F32)],
        compiler_params=_params("parallel", "arbitrary"),
        name="fox_cumsum",
    )(gates)


def _bias_lanes(col, own_first):
    n = col.shape[0]
    hi, mid, lo = (p.astype(F32) for p in _split3(col))
    lane = lax.broadcasted_iota(jnp.int32, (n, LANES), 1)
    base = 0 if own_first else 3
    ones = ((lane >= 3 - base) & (lane < 6 - base)).astype(F32)
    out = jnp.where(lane == base, hi, jnp.where(lane == base + 1, mid, jnp.where(lane == base + 2, lo, ones)))
    return out.astype(BF16)


def _fox_attn_kernel(q_ref, k_ref, v_ref, cc_ref, o_ref, ka_scr, vt_scr, s_scr, p_scr,
                     a_scr, m_scr, l_scr, acc_scr, *, tq, seq, hps):
    hp = pl.program_id(1)
    i = pl.program_id(2)
    dh = LANES
    local_heads = range(hps)

    def head_col(rows, hl):
        lane = lax.broadcasted_iota(jnp.int32, rows.shape, 1)
        return jnp.sum(jnp.where(lane == hp * hps + hl, rows, 0.0), axis=1, keepdims=True) * LOG2E

    @pl.when(i == 0)
    def _():
        for hl in local_heads:
            hc = slice(hl * dh, (hl + 1) * dh)
            for r0 in range(0, seq, tq):
                ka_scr[hl, r0:r0 + tq, :LANES] = k_ref[r0:r0 + tq, hc]
                ka_scr[hl, r0:r0 + tq, LANES:] = _bias_lanes(-head_col(cc_ref[r0:r0 + tq, :], hl), True)
                vt_scr[hl, :, r0:r0 + tq] = v_ref[r0:r0 + tq, hc].astype(F32).T.astype(BF16)

    q0 = pl.multiple_of(i * tq, tq)
    cum_q = cc_ref[pl.ds(q0, tq), :]
    qa = [jnp.concatenate([q_ref[:, hl * dh:(hl + 1) * dh], _bias_lanes(head_col(cum_q, hl), False)], axis=1)
          for hl in local_heads]

    m_scr[...] = jnp.full_like(m_scr, NEG_BIG)
    l_scr[...] = jnp.zeros_like(l_scr)
    acc_scr[...] = jnp.zeros_like(acc_scr)
    a_scr[...] = jnp.ones_like(a_scr)
    for hl in local_heads:
        p_scr[hl, 1] = jnp.zeros((tq, tq), BF16)

    def logits(j, parity):
        start = pl.multiple_of(j * tq, tq)
        for hl in local_heads:
            s_scr[hl, parity] = _dot_nt(ka_scr[hl, pl.ds(start, tq), :], qa[hl])

    def softmax(parity, on_diagonal):
        for hl in local_heads:
            st = s_scr[hl, parity]
            if on_diagonal:
                ki = lax.broadcasted_iota(jnp.int32, (tq, tq), 0)
                qi = lax.broadcasted_iota(jnp.int32, (tq, tq), 1)
                st = jnp.where(ki <= qi, st, NEG_BIG)
            m_old = m_scr[hl]
            m_new = jnp.maximum(m_old, jnp.max(st, axis=0, keepdims=True))
            p = jnp.exp2(st - m_new)
            a = jnp.exp2(m_old - m_new)
            l_scr[hl] = a * l_scr[hl] + jnp.sum(p, axis=0, keepdims=True)
            m_scr[hl] = m_new
            p_scr[hl, parity] = p.astype(BF16)
            a_scr[hl] = a

    def values(j, parity):
        start = pl.multiple_of(jnp.maximum(j, 0) * tq, tq)
        for hl in local_heads:
            acc_scr[hl] = a_scr[hl] * acc_scr[hl] + _dot(vt_scr[hl, :, pl.ds(start, tq)], p_scr[hl, parity])

    def step(j, parity):
        values(j - 1, 1 - parity)
        logits(j + 1, 1 - parity)
        softmax(parity, False)

    def finish(parity):
        values(i - 1, 1 - parity)
        softmax(parity, True)
        values(i, parity)

    logits(0, 0)

    def pair(t, carry):
        step(2 * t, 0)
        step(2 * t + 1, 1)
        return carry

    lax.fori_loop(0, i // 2, pair, 0)

    @pl.when(i % 2 == 0)
    def _():
        finish(0)

    @pl.when(i % 2 == 1)
    def _():
        step(i - 1, 0)
        finish(1)

    for hl in local_heads:
        o_ref[:, hl * dh:(hl + 1) * dh] = (acc_scr[hl] * (1.0 / l_scr[hl])).T.astype(o_ref.dtype)


def _fox_attn(qkv, cum, batch, seq):
    dh = FOX_HEAD_DIM
    assert dh == LANES
    m = qkv.shape[0]
    heads = qkv.shape[1] // (3 * dh)
    assert heads <= LANES
    hps = FOX_HEADS_PER_STEP if heads % FOX_HEADS_PER_STEP == 0 else 1
    groups = heads // hps
    tq = _tile(seq, FOX_TQ)
    nq = seq // tq
    return pl.pallas_call(
        functools.partial(_fox_attn_kernel, tq=tq, seq=seq, hps=hps),
        grid=(batch, groups, nq),
        in_specs=[pl.BlockSpec((tq, hps * dh), lambda b, h, i: (b * nq + i, h)),
                  pl.BlockSpec((seq, hps * dh), lambda b, h, i: (b, groups + h)),
                  pl.BlockSpec((seq, hps * dh), lambda b, h, i: (b, 2 * groups + h)),
                  pl.BlockSpec((seq, LANES), lambda b, h, i: (b, 0))],
        out_specs=pl.BlockSpec((tq, hps * dh), lambda b, h, i: (b * nq + i, h)),
        out_shape=jax.ShapeDtypeStruct((m, heads * dh), BF16),
        scratch_shapes=[pltpu.VMEM((hps, seq, 2 * LANES), BF16), pltpu.VMEM((hps, dh, seq), BF16),
                        pltpu.VMEM((hps, 2, tq, tq), F32), pltpu.VMEM((hps, 2, tq, tq), BF16),
                        pltpu.VMEM((hps, 1, tq), F32), pltpu.VMEM((hps, 1, tq), F32),
                        pltpu.VMEM((hps, 1, tq), F32), pltpu.VMEM((hps, dh, tq), F32)],
        compiler_params=_params("parallel", "parallel", "arbitrary"),
        name="fox_attn",
    )(qkv, qkv, qkv, cum)


def kernel(x, mlstm_w_in, mlstm_b_gate, mlstm_norm_g, mlstm_w_out, fox_w_in, fox_b_f, fox_w_out,
           ln_mix_g, ln_mix_b, ffn_w_up, ffn_w_down, ln_ffn_g, ln_ffn_b):
    batch, seq, d = x.shape
    depth = ln_mix_g.shape[0]
    alpha = (2 * depth) ** 0.25
    m = batch * seq
    ffn = ffn_w_down.shape[1]
    ffn_pad = -(-ffn // FFN_PAD) * FFN_PAD
    hv = mlstm_w_out.shape[1]
    mlstm_main = mlstm_w_in.shape[2] - 2 * MLSTM_HEADS
    mlstm_qk = (mlstm_main - 2 * hv) // 2
    mlstm_dk = mlstm_qk // MLSTM_HEADS
    fox_w = fox_w_out.shape[1]

    mlstm_wt = jnp.swapaxes(mlstm_w_in, 1, 2)
    fox_wt = jnp.swapaxes(fox_w_in, 1, 2)

    h32 = x.reshape(m, d)
    hbf = h32.astype(BF16)
    for layer in range(depth):
        slot = layer // N_MIXERS
        if layer % N_MIXERS == 0:
            qkv = _proj(hbf, mlstm_wt, slot, 0, mlstm_main - hv, BF16, mlstm_qk, mlstm_dk ** -0.5)
            o = _proj(hbf, mlstm_wt, slot, mlstm_main - hv, hv, F32)
            gates = _gate_proj(h32, mlstm_wt, slot, mlstm_main, mlstm_b_gate[slot])
            mixed = _mlstm_core(qkv, o, gates, gates.T, mlstm_norm_g[slot], batch, seq)
            w_out = _cast_pad_rows(mlstm_w_out, slot, hv)
        else:
            qkv = _proj(hbf, fox_wt, slot, 0, 3 * fox_w, BF16, fox_w, FOX_HEAD_DIM ** -0.5 * LOG2E)
            gates = _gate_proj(h32, fox_wt, slot, 3 * fox_w, fox_b_f[slot])
            mixed = _fox_attn(qkv, _fox_cumsum(gates, batch, seq), batch, seq)
            w_out = _cast_pad_rows(fox_w_out, slot, fox_w)
        h32, hbf = _mm_res_ln(mixed, w_out, h32, ln_mix_g[layer], ln_mix_b[layer], alpha)

        hidden = _ffn_up(hbf, ffn_w_up, layer, ffn_pad)
        w_down = _cast_pad_rows(ffn_w_down, layer, ffn_pad)
        h32, hbf = _mm_res_ln(hidden, w_down, h32, ln_ffn_g[layer], ln_ffn_b[layer], alpha)
    return h32.reshape(batch, seq, d)
```

```python
import functools
import math

import jax
import jax.numpy as jnp
from jax import lax
from jax.experimental import pallas as pl
from jax.experimental.pallas import tpu as pltpu

F32 = jnp.float32
BF16 = jnp.bfloat16

MLSTM_HEADS = 8
FOX_HEAD_DIM = 128
GATE_SOFT_CAP = 15.0
HEAD_NORM_EPS = 1e-6
LN_EPS = 1e-5
N_MIXERS = 2

LANES = 128
SUBLANES = 8
NEG_BIG = -1e30
LOG2E = math.log2(math.e)

BIG_TM = 2048
MM_TN = 512
FFN_TN = 256
LN_TM = 1024
LN_OUT_ROWS = 128
LN_TK = 1024
LN_H_CHUNKS = 8
LN_NCHUNK = 1024
LN_ROWS = 16
GATE_TM = 512
FFN_PAD = 1024
MLSTM_CHUNK = 256
MLSTM_HEADS_PER_STEP = 8
FOX_TQ = 512
FOX_TS = 512
FOX_HEADS_PER_STEP = 4
VMEM_LIMIT = 56 * 1024 * 1024


def _tile(n, pref):
    if n <= pref:
        return n
    t = (pref // LANES) * LANES
    while t >= LANES:
        if n % t == 0:
            return t
        t -= LANES
    raise ValueError(f"no lane-aligned tile of {n} below {pref}")


def _params(*sem):
    return pltpu.CompilerParams(dimension_semantics=sem, vmem_limit_bytes=VMEM_LIMIT)


def _dot(a, b):
    return jnp.dot(a, b, preferred_element_type=F32)


def _dot_nt(a, b):
    return lax.dot_general(a, b, (((1,), (1,)), ((), ())), preferred_element_type=F32)


def _dot_tn(a, b):
    return lax.dot_general(a, b, (((0,), (0,)), ((), ())), preferred_element_type=F32)


def _split3(x):
    hi = x.astype(BF16)
    r1 = x - hi.astype(F32)
    mid = r1.astype(BF16)
    lo = (r1 - mid.astype(F32)).astype(BF16)
    return hi, mid, lo


def _sigmoid(x):
    return 1.0 / (1.0 + jnp.exp(-x))


def _log_sigmoid(x):
    return jnp.minimum(x, 0.0) - jnp.log1p(jnp.exp(-jnp.abs(x)))


def _soft_cap(z):
    return GATE_SOFT_CAP * jnp.tanh(z / GATE_SOFT_CAP)


def _proj_kernel(x_ref, w_ref, o_ref, *, n_scaled, scale):
    j = pl.program_id(1)
    acc = _dot_nt(x_ref[...], w_ref[...].astype(BF16))
    if n_scaled:
        acc = acc * jnp.where(j < n_scaled, scale, 1.0)
    o_ref[...] = acc.astype(o_ref.dtype)


def _proj_side_kernel(x_ref, w_ref, side_ref, o_ref, side_bf_ref, *, n_scaled, scale):
    _proj_kernel(x_ref, w_ref, o_ref, n_scaled=n_scaled, scale=scale)
    side_bf_ref[...] = side_ref[...].astype(BF16)


def _proj(x, wt_stack, idx, col0, ncols, out_dtype, scaled_cols=0, scale=1.0, side_stack=None):
    m, k = x.shape
    tm, tn = _tile(m, BIG_TM), _tile(ncols, MM_TN)
    assert col0 % tn == 0 and scaled_cols % tn == 0
    j0 = col0 // tn
    ni, nj = m // tm, ncols // tn
    in_specs = [pl.BlockSpec((tm, k), lambda i, j: (i, 0), pipeline_mode=pl.Buffered(1)),
                pl.BlockSpec((None, tn, k), lambda i, j: (idx, j0 + j, 0))]
    out_specs = [pl.BlockSpec((tm, tn), lambda i, j: (i, j))]
    out_shape = [jax.ShapeDtypeStruct((m, ncols), out_dtype)]
    operands = [x, wt_stack]
    if side_stack is not None:
        _, rows, cols = side_stack.shape
        sr = next(r for r in range(2 * SUBLANES, rows + 1, 2 * SUBLANES)
                  if rows % r == 0 and r * ni * nj >= rows)
        nblocks = rows // sr
        block = lambda i, j: jnp.minimum(i * nj + j, nblocks - 1)
        in_specs.append(pl.BlockSpec((None, sr, cols), lambda i, j: (idx, block(i, j), 0)))
        out_specs.append(pl.BlockSpec((sr, cols), lambda i, j: (block(i, j), 0)))
        out_shape.append(jax.ShapeDtypeStruct((rows, cols), BF16))
        operands.append(side_stack)
    body = _proj_kernel if side_stack is None else _proj_side_kernel
    res = pl.pallas_call(
        functools.partial(body, n_scaled=scaled_cols // tn, scale=scale),
        grid=(ni, nj),
        in_specs=in_specs,
        out_specs=out_specs,
        out_shape=out_shape,
        compiler_params=_params("arbitrary", "arbitrary"),
        name="proj",
    )(*operands)
    return res if side_stack is not None else res[0]


def _gate_kernel(h_ref, w_ref, b_ref, o_ref, *, n_gates):
    x = h_ref[...]
    w = w_ref[...]
    x_hi = x.astype(BF16)
    x_lo = (x - x_hi.astype(F32)).astype(BF16)
    w_hi = w.astype(BF16)
    w_lo = (w - w_hi.astype(F32)).astype(BF16)
    acc = _dot_nt(x_hi, w_hi) + _dot_nt(x_lo, w_hi) + _dot_nt(x_hi, w_lo)
    lane = lax.broadcasted_iota(jnp.int32, acc.shape, 1)
    o_ref[...] = jnp.where(lane < n_gates, acc + b_ref[...], 0.0)


def _gate_proj(h32, wt_stack, idx, col0, bias):
    m, d = h32.shape
    g = wt_stack.shape[1] - col0
    assert col0 % LANES == 0 and 0 < g <= LANES
    b_pad = jnp.pad(bias.astype(F32), (0, LANES - g)).reshape(1, LANES)
    tm = _tile(m, GATE_TM)
    return pl.pallas_call(
        functools.partial(_gate_kernel, n_gates=g),
        grid=(m // tm,),
        in_specs=[pl.BlockSpec((tm, d), lambda i: (i, 0)),
                  pl.BlockSpec((None, LANES, d), lambda i: (idx, col0 // LANES, 0)),
                  pl.BlockSpec((1, LANES), lambda i: (0, 0))],
        out_specs=pl.BlockSpec((tm, LANES), lambda i: (i, 0)),
        out_shape=jax.ShapeDtypeStruct((m, LANES), F32),
        compiler_params=_params("parallel"),
        name="gate_proj",
    )(h32, wt_stack, b_pad)


def _mm_res_ln_kernel(a_ref, b_ref, h_ref, g_ref, beta_ref, o32_ref, obf_ref,
                      acc_ref, part_scr, mu_scr, rstd_scr, *, alpha, nk, nh):
    t = pl.program_id(1)
    tm, d = acc_ref.shape
    rows_out = o32_ref.shape[0]
    nch = min(d, LN_NCHUNK)
    chunks = range(0, d, nch)

    @pl.when(t == 0)
    def _():
        a = a_ref[...]
        for n0 in chunks:
            acc_ref[:, n0:n0 + nch] = _dot(a, b_ref[:, n0:n0 + nch])

    @pl.when((t > 0) & (t < nk))
    def _():
        a = a_ref[...]
        for n0 in chunks:
            acc_ref[:, n0:n0 + nch] += _dot(a, b_ref[:, n0:n0 + nch])

    @pl.when(t < nh)
    def _():
        hc = d // nh
        cols = pl.ds(pl.multiple_of(t * hc, hc), hc)
        acc_ref[:, cols] += alpha * h_ref[...]

    @pl.when(t >= nk)
    def _():
        rb = min(rows_out, LN_ROWS)
        row0 = (t - nk) * rows_out
        cols = [slice(c, c + LANES) for c in range(0, d, LANES)]

        def row_groups(fn):
            def body(r, carry):
                local = pl.multiple_of(r * rb, rb)
                fn(pl.ds(local, rb), pl.ds(pl.multiple_of(row0 + local, rb), rb))
                return carry
            lax.fori_loop(0, rows_out // rb, body, 0)

        def lane_mean(part):
            total = jnp.sum(part, axis=1, keepdims=True) * (1.0 / d)
            return jnp.broadcast_to(total, part.shape)

        def tree_sum(xs):
            while len(xs) > 1:
                xs = [xs[n] + xs[n + 1] for n in range(0, len(xs) - 1, 2)] + xs[len(xs) & ~1:]
            return xs[0]

        def pass_sum(rows, src):
            part_scr[rows, :] = tree_sum([acc_ref[src, c] for c in cols])

        def pass_sq(rows, src):
            mu = mu_scr[rows, :]
            dev = [acc_ref[src, c] - mu for c in cols]
            part_scr[rows, :] = tree_sum([x * x for x in dev])

        def pass_norm(rows, src):
            mu = mu_scr[rows, :]
            rstd = rstd_scr[rows, :]
            for c in cols:
                out = (acc_ref[src, c] - mu) * rstd * g_ref[:, c] + beta_ref[:, c]
                o32_ref[rows, c] = out
                obf_ref[rows, c] = out.astype(BF16)

        row_groups(pass_sum)
        mu_scr[...] = lane_mean(part_scr[...])
        row_groups(pass_sq)
        rstd_scr[...] = lax.rsqrt(lane_mean(part_scr[...]) + LN_EPS)
        row_groups(pass_norm)


def _mm_res_ln(a, b, h32, g, beta, alpha):
    m, kdim = a.shape
    d = b.shape[1]
    tm, tk = _tile(m, LN_TM), _tile(kdim, LN_TK)
    nk = kdim // tk
    nh = max(n for n in range(1, min(nk, LN_H_CHUNKS) + 1) if d % (n * LANES) == 0)
    rows_out = min(tm, LN_OUT_ROWS)
    ns = tm // rows_out
    ni = m // tm
    sweep = lambda t: t >= nk
    i_of = lambda i, t: jnp.where(sweep(t), jnp.minimum(i + 1, ni - 1), i)
    k_of = lambda t: jnp.where(sweep(t), 0, t)
    out_row = lambda i, t: (i * ns + jnp.maximum(t - nk, 0), 0)
    return pl.pallas_call(
        functools.partial(_mm_res_ln_kernel, alpha=alpha, nk=nk, nh=nh),
        grid=(ni, nk + ns),
        in_specs=[pl.BlockSpec((tm, tk), lambda i, t: (i_of(i, t), k_of(t))),
                  pl.BlockSpec((tk, d), lambda i, t: (k_of(t), 0)),
                  pl.BlockSpec((tm, d // nh), lambda i, t: (i_of(i, t), jnp.minimum(k_of(t), nh - 1))),
                  pl.BlockSpec((1, d), lambda i, t: (0, 0)),
                  pl.BlockSpec((1, d), lambda i, t: (0, 0))],
        out_specs=[pl.BlockSpec((rows_out, d), out_row),
                   pl.BlockSpec((rows_out, d), out_row)],
        out_shape=[jax.ShapeDtypeStruct((m, d), F32), jax.ShapeDtypeStruct((m, d), BF16)],
        scratch_shapes=[pltpu.VMEM((tm, d), F32)] + [pltpu.VMEM((rows_out, LANES), F32)] * 3,
        compiler_params=_params("parallel", "arbitrary"),
        name="proj_res_ln",
    )(a, b, h32, g.reshape(1, d).astype(F32), beta.reshape(1, d).astype(F32))


def _ffn_up_kernel(x_ref, wg_ref, wu_ref, wd_ref, o_ref, wd_bf_ref, *, n_real, nj, wd_blocks):
    i = pl.program_id(0)
    j = pl.program_id(1)

    def cast_down_rows():
        real_rows = i * nj + j < wd_blocks
        wd_bf_ref[...] = jnp.where(real_rows, wd_ref[...], 0.0).astype(BF16)

    @pl.when(j < n_real)
    def _():
        x = x_ref[...]
        gate = _dot(x, wg_ref[...].astype(BF16))
        up = _dot(x, wu_ref[...].astype(BF16))
        o_ref[...] = (gate * _sigmoid(gate) * up).astype(o_ref.dtype)
        cast_down_rows()

    @pl.when(j >= n_real)
    def _():
        o_ref[...] = jnp.zeros_like(o_ref)
        cast_down_rows()


def _ffn_up(x, w_up_stack, w_down_stack, idx, ffn_pad):
    m, d = x.shape
    f = w_up_stack.shape[2] // 2
    d_out = w_down_stack.shape[2]
    tm, tn = _tile(m, BIG_TM), _tile(f, FFN_TN)
    assert ffn_pad % tn == 0
    n_real = f // tn
    ni, nj = m // tm, ffn_pad // tn
    wd_rows = ffn_pad // (ni * nj)
    assert wd_rows * ni * nj == ffn_pad and wd_rows % (2 * SUBLANES) == 0 and f % wd_rows == 0
    wd_blocks = f // wd_rows
    real = lambda j: jnp.minimum(j, n_real - 1)
    return pl.pallas_call(
        functools.partial(_ffn_up_kernel, n_real=n_real, nj=nj, wd_blocks=wd_blocks),
        grid=(ni, nj),
        in_specs=[pl.BlockSpec((tm, d), lambda i, j: (i, 0), pipeline_mode=pl.Buffered(1)),
                  pl.BlockSpec((None, d, tn), lambda i, j: (idx, 0, real(j))),
                  pl.BlockSpec((None, d, tn), lambda i, j: (idx, 0, n_real + real(j))),
                  pl.BlockSpec((None, wd_rows, d_out),
                               lambda i, j: (idx, jnp.minimum(i * nj + j, wd_blocks - 1), 0))],
        out_specs=[pl.BlockSpec((tm, tn), lambda i, j: (i, j)),
                   pl.BlockSpec((wd_rows, d_out), lambda i, j: (i * nj + j, 0))],
        out_shape=[jax.ShapeDtypeStruct((m, ffn_pad), BF16),
                   jax.ShapeDtypeStruct((ffn_pad, d_out), BF16)],
        compiler_params=_params("arbitrary", "arbitrary"),
        name="ffn_up",
    )(x, w_up_stack, w_up_stack, w_down_stack)


def _mlstm_kernel(q_ref, k_ref, v_ref, o_ref, gc_ref, gri_ref, grf_ref, ng_ref, out_ref,
                  ct_ref, m_ref, *, chunk, dk, dv, heads, hps):
    hp = pl.program_id(1)
    c = pl.program_id(2)
    L = chunk

    @pl.when(c == 0)
    def _():
        ct_ref[...] = jnp.zeros_like(ct_ref)
        m_ref[...] = jnp.zeros_like(m_ref)

    gc = gc_ref[...]
    lane = lax.broadcasted_iota(jnp.int32, (L, LANES), 1)
    ti = lax.broadcasted_iota(jnp.int32, (L, L), 0)
    si = lax.broadcasted_iota(jnp.int32, (L, L), 1)
    causal = si <= ti
    tri_lo = causal.astype(BF16)
    tri_up = (ti <= si).astype(BF16)
    one_lane = (lane == 0).astype(F32)

    for hl in range(hps):
        h = hp * hps + hl
        kcols = slice(hl * dk, (hl + 1) * dk)
        vcols = slice(hl * dv, (hl + 1) * dv)

        ig_c = jnp.sum(jnp.where(lane == h, gc, 0.0), axis=1, keepdims=True)
        fg_c = jnp.sum(jnp.where(lane == heads + h, gc, 0.0), axis=1, keepdims=True)
        ig_r = gri_ref[pl.ds(h, 1), :]
        fg_r = grf_ref[pl.ds(h, 1), :]
        ig_c, ig_r = _soft_cap(ig_c), _soft_cap(ig_r)
        lf_c, lf_r = _log_sigmoid(_soft_cap(fg_c)), _log_sigmoid(_soft_cap(fg_r))

        b_c = sum(_dot(tri_lo, p) for p in _split3(jnp.broadcast_to(lf_c, (L, LANES))))[:, :1]
        b_r = sum(_dot(p, tri_up) for p in _split3(jnp.broadcast_to(lf_r, (SUBLANES, L))))[:1, :]

        m_prev = m_ref[hl]
        d_log = jnp.where(causal, b_c - b_r + ig_r, NEG_BIG)
        g_log = b_c + m_prev
        m_t = jnp.maximum(g_log, jnp.max(d_log, axis=1, keepdims=True))
        w_intra = jnp.exp(d_log - m_t)
        w_inter = jnp.exp(g_log - m_t)

        q = q_ref[:, kcols]
        k = k_ref[:, kcols]
        v = v_ref[:, vcols]
        s = _dot_nt(q, k) * w_intra
        ct = ct_ref[hl]
        inter = _dot(q, ct.astype(BF16))
        num = _dot(s.astype(BF16), v) + w_inter * inter[:, :dv]
        den = jnp.sum(s, axis=1, keepdims=True) + w_inter * inter[:, dv:dv + 1]
        hh = num * (1.0 / jnp.maximum(jnp.abs(den), jnp.exp(-m_t)))

        b_end = b_c[L - 1:L, :]
        a_c = b_end - b_c + ig_c
        a_r = b_end - b_r + ig_r
        m_new = jnp.maximum(b_end + m_prev, jnp.max(a_r, axis=1, keepdims=True))
        decay = jnp.exp(b_end + m_prev - m_new)
        wa_c = jnp.exp(a_c - m_new)
        wv = jnp.concatenate([wa_c * v.astype(F32), wa_c * one_lane], axis=1).astype(BF16)
        ct_ref[hl] = decay * ct + _dot_tn(k, wv)
        m_ref[hl] = m_new

        hn = hh * lax.rsqrt(jnp.mean(hh * hh, axis=1, keepdims=True) + HEAD_NORM_EPS)
        out_ref[:, vcols] = (hn * ng_ref[:, vcols] * _sigmoid(o_ref[:, vcols])).astype(out_ref.dtype)


def _mlstm_core(qkv, o, gates, gates_t, norm_g, batch, seq):
    heads = MLSTM_HEADS
    assert heads == SUBLANES, "row-form gate blocks assume one sublane tile per gate kind"
    m = qkv.shape[0]
    hv = o.shape[1]
    dv = hv // heads
    dk = (qkv.shape[1] - hv) // (2 * heads)
    assert 2 * heads * dk == hv, "v blocks are addressed in units of dv"
    hps = MLSTM_HEADS_PER_STEP
    assert heads % hps == 0
    groups = heads // hps
    chunk = _tile(seq, MLSTM_CHUNK)
    nc = seq // chunk
    row = lambda b, h, c: b * nc + c
    return pl.pallas_call(
        functools.partial(_mlstm_kernel, chunk=chunk, dk=dk, dv=dv, heads=heads, hps=hps),
        grid=(batch, groups, nc),
        in_specs=[pl.BlockSpec((chunk, hps * dk), lambda b, h, c: (row(b, h, c), h)),
                  pl.BlockSpec((chunk, hps * dk), lambda b, h, c: (row(b, h, c), groups + h)),
                  pl.BlockSpec((chunk, hps * dv), lambda b, h, c: (row(b, h, c), groups + h)),
                  pl.BlockSpec((chunk, hps * dv), lambda b, h, c: (row(b, h, c), h)),
                  pl.BlockSpec((chunk, LANES), lambda b, h, c: (row(b, h, c), 0)),
                  pl.BlockSpec((SUBLANES, chunk), lambda b, h, c: (0, row(b, h, c))),
                  pl.BlockSpec((SUBLANES, chunk), lambda b, h, c: (1, row(b, h, c))),
                  pl.BlockSpec((1, hps * dv), lambda b, h, c: (0, h))],
        out_specs=pl.BlockSpec((chunk, hps * dv), lambda b, h, c: (row(b, h, c), h)),
        out_shape=jax.ShapeDtypeStruct((m, hv), BF16),
        scratch_shapes=[pltpu.VMEM((hps, dk, dv + LANES), F32), pltpu.VMEM((hps, 1, 1), F32)],
        compiler_params=_params("parallel", "parallel", "arbitrary"),
        name="mlstm_core",
    )(qkv, qkv, qkv, o, gates, gates_t, gates_t, norm_g.reshape(1, hv).astype(F32))


def _fox_cum_kernel(g_ref, cum_ref, carry_ref, *, ts):
    @pl.when(pl.program_id(1) == 0)
    def _():
        carry_ref[...] = jnp.zeros_like(carry_ref)

    logf = _log_sigmoid(g_ref[...])
    ti = lax.broadcasted_iota(jnp.int32, (ts, ts), 0)
    si = lax.broadcasted_iota(jnp.int32, (ts, ts), 1)
    tri_lo = (si <= ti).astype(BF16)
    cum = sum(_dot(tri_lo, p) for p in _split3(logf)) + carry_ref[...]
    carry_ref[...] = cum[ts - 1:ts, :]
    cum_ref[...] = cum


def _fox_cumsum(gates, batch, seq):
    ts = _tile(seq, FOX_TS)
    ns = seq // ts
    return pl.pallas_call(
        functools.partial(_fox_cum_kernel, ts=ts),
        grid=(batch, ns),
        in_specs=[pl.BlockSpec((ts, LANES), lambda b, s: (b * ns + s, 0))],
        out_specs=pl.BlockSpec((ts, LANES), lambda b, s: (b * ns + s, 0)),
        out_shape=jax.ShapeDtypeStruct((batch * seq, LANES), F32),
        scratch_shapes=[pltpu.VMEM((1, LANES), F32)],
        compiler_params=_params("parallel", "arbitrary"),
        name="fox_cumsum",
    )(gates)


def _bias_lanes(col, own_first):
    n = col.shape[0]
    hi, mid, lo = (p.astype(F32) for p in _split3(col))
    lane = lax.broadcasted_iota(jnp.int32, (n, LANES), 1)
    base = 0 if own_first else 3
    ones = ((lane >= 3 - base) & (lane < 6 - base)).astype(F32)
    out = jnp.where(lane == base, hi, jnp.where(lane == base + 1, mid, jnp.where(lane == base + 2, lo, ones)))
    return out.astype(BF16)


def _fox_attn_kernel(q_ref, k_ref, v_ref, cc_ref, o_ref, ka_scr, vt_scr, s_scr, p_scr,
                     a_scr, m_scr, l_scr, acc_scr, *, tq, seq, hps):
    hp = pl.program_id(1)
    i = pl.program_id(2)
    dh = LANES
    local_heads = range(hps)

    def head_col(rows, hl):
        lane = lax.broadcasted_iota(jnp.int32, rows.shape, 1)
        return jnp.sum(jnp.where(lane == hp * hps + hl, rows, 0.0), axis=1, keepdims=True) * LOG2E

    @pl.when(i == 0)
    def _():
        for hl in local_heads:
            hc = slice(hl * dh, (hl + 1) * dh)
            for r0 in range(0, seq, tq):
                ka_scr[hl, r0:r0 + tq, :LANES] = k_ref[r0:r0 + tq, hc]
                ka_scr[hl, r0:r0 + tq, LANES:] = _bias_lanes(-head_col(cc_ref[r0:r0 + tq, :], hl), True)
                vt_scr[hl, :, r0:r0 + tq] = v_ref[r0:r0 + tq, hc].astype(F32).T.astype(BF16)

    q0 = pl.multiple_of(i * tq, tq)
    cum_q = cc_ref[pl.ds(q0, tq), :]
    qa = [jnp.concatenate([q_ref[:, hl * dh:(hl + 1) * dh], _bias_lanes(head_col(cum_q, hl), False)], axis=1)
          for hl in local_heads]

    m_scr[...] = jnp.full_like(m_scr, NEG_BIG)
    l_scr[...] = jnp.zeros_like(l_scr)
    acc_scr[...] = jnp.zeros_like(acc_scr)
    a_scr[...] = jnp.ones_like(a_scr)
    for hl in local_heads:
        p_scr[hl, 1] = jnp.zeros((tq, tq), BF16)

    def logits(j, parity):
        start = pl.multiple_of(j * tq, tq)
        for hl in local_heads:
            s_scr[hl, parity] = _dot_nt(ka_scr[hl, pl.ds(start, tq), :], qa[hl])

    def softmax(parity, on_diagonal):
        for hl in local_heads:
            st = s_scr[hl, parity]
            if on_diagonal:
                ki = lax.broadcasted_iota(jnp.int32, (tq, tq), 0)
                qi = lax.broadcasted_iota(jnp.int32, (tq, tq), 1)
                st = jnp.where(ki <= qi, st, NEG_BIG)
            m_old = m_scr[hl]
            m_new = jnp.maximum(m_old, jnp.max(st, axis=0, keepdims=True))
            p = jnp.exp2(st - m_new)
            a = jnp.exp2(m_old - m_new)
            l_scr[hl] = a * l_scr[hl] + jnp.sum(p, axis=0, keepdims=True)
            m_scr[hl] = m_new
            p_scr[hl, parity] = p.astype(BF16)
            a_scr[hl] = a

    def values(j, parity):
        start = pl.multiple_of(jnp.maximum(j, 0) * tq, tq)
        for hl in local_heads:
            acc_scr[hl] = a_scr[hl] * acc_scr[hl] + _dot(vt_scr[hl, :, pl.ds(start, tq)], p_scr[hl, parity])

    def step(j, parity):
        values(j - 1, 1 - parity)
        logits(j + 1, 1 - parity)
        softmax(parity, False)

    def finish(parity):
        values(i - 1, 1 - parity)
        softmax(parity, True)
        values(i, parity)

    logits(0, 0)

    def pair(t, carry):
        step(2 * t, 0)
        step(2 * t + 1, 1)
        return carry

    lax.fori_loop(0, i // 2, pair, 0)

    @pl.when(i % 2 == 0)
    def _():
        finish(0)

    @pl.when(i % 2 == 1)
    def _():
        step(i - 1, 0)
        finish(1)

    for hl in local_heads:
        o_ref[:, hl * dh:(hl + 1) * dh] = (acc_scr[hl] * (1.0 / l_scr[hl])).T.astype(o_ref.dtype)


def _fox_attn(qkv, cum, batch, seq):
    dh = FOX_HEAD_DIM
    assert dh == LANES
    m = qkv.shape[0]
    heads = qkv.shape[1] // (3 * dh)
    assert heads <= LANES
    hps = FOX_HEADS_PER_STEP if heads % FOX_HEADS_PER_STEP == 0 else 1
    groups = heads // hps
    tq = _tile(seq, FOX_TQ)
    nq = seq // tq
    return pl.pallas_call(
        functools.partial(_fox_attn_kernel, tq=tq, seq=seq, hps=hps),
        grid=(batch, groups, nq),
        in_specs=[pl.BlockSpec((tq, hps * dh), lambda b, h, i: (b * nq + i, h)),
                  pl.BlockSpec((seq, hps * dh), lambda b, h, i: (b, groups + h)),
                  pl.BlockSpec((seq, hps * dh), lambda b, h, i: (b, 2 * groups + h)),
                  pl.BlockSpec((seq, LANES), lambda b, h, i: (b, 0))],
        out_specs=pl.BlockSpec((tq, hps * dh), lambda b, h, i: (b * nq + i, h)),
        out_shape=jax.ShapeDtypeStruct((m, heads * dh), BF16),
        scratch_shapes=[pltpu.VMEM((hps, seq, 2 * LANES), BF16), pltpu.VMEM((hps, dh, seq), BF16),
                        pltpu.VMEM((hps, 2, tq, tq), F32), pltpu.VMEM((hps, 2, tq, tq), BF16),
                        pltpu.VMEM((hps, 1, tq), F32), pltpu.VMEM((hps, 1, tq), F32),
                        pltpu.VMEM((hps, 1, tq), F32), pltpu.VMEM((hps, dh, tq), F32)],
        compiler_params=_params("parallel", "parallel", "arbitrary"),
        name="fox_attn",
    )(qkv, qkv, qkv, cum)


def kernel(x, mlstm_w_in, mlstm_b_gate, mlstm_norm_g, mlstm_w_out, fox_w_in, fox_b_f, fox_w_out,
           ln_mix_g, ln_mix_b, ffn_w_up, ffn_w_down, ln_ffn_g, ln_ffn_b):
    batch, seq, d = x.shape
    depth = ln_mix_g.shape[0]
    alpha = (2 * depth) ** 0.25
    m = batch * seq
    ffn = ffn_w_down.shape[1]
    ffn_pad = -(-ffn // FFN_PAD) * FFN_PAD
    hv = mlstm_w_out.shape[1]
    mlstm_main = mlstm_w_in.shape[2] - 2 * MLSTM_HEADS
    mlstm_qk = (mlstm_main - 2 * hv) // 2
    mlstm_dk = mlstm_qk // MLSTM_HEADS
    fox_w = fox_w_out.shape[1]

    mlstm_wt = jnp.swapaxes(mlstm_w_in, 1, 2)
    fox_wt = jnp.swapaxes(fox_w_in, 1, 2)

    h32 = x.reshape(m, d)
    hbf = h32.astype(BF16)
    for layer in range(depth):
        slot = layer // N_MIXERS
        if layer % N_MIXERS == 0:
            qkv, w_out = _proj(hbf, mlstm_wt, slot, 0, mlstm_main - hv, BF16, mlstm_qk, mlstm_dk ** -0.5,
                               side_stack=mlstm_w_out)
            o = _proj(hbf, mlstm_wt, slot, mlstm_main - hv, hv, F32)
            gates = _gate_proj(h32, mlstm_wt, slot, mlstm_main, mlstm_b_gate[slot])
            mixed = _mlstm_core(qkv, o, gates, gates.T, mlstm_norm_g[slot], batch, seq)
        else:
            qkv, w_out = _proj(hbf, fox_wt, slot, 0, 3 * fox_w, BF16, fox_w, FOX_HEAD_DIM ** -0.5 * LOG2E,
                               side_stack=fox_w_out)
            gates = _gate_proj(h32, fox_wt, slot, 3 * fox_w, fox_b_f[slot])
            mixed = _fox_attn(qkv, _fox_cumsum(gates, batch, seq), batch, seq)
        h32, hbf = _mm_res_ln(mixed, w_out, h32, ln_mix_g[layer], ln_mix_b[layer], alpha)

        hidden, w_down = _ffn_up(hbf, ffn_w_up, ffn_w_down, layer, ffn_pad)
        h32, hbf = _mm_res_ln(hidden, w_down, h32, ln_ffn_g[layer], ln_ffn_b[layer], alpha)
    return h32.reshape(batch, seq, d)
```

```python
import functools
import math

import jax
import jax.numpy as jnp
from jax import lax
from jax.experimental import pallas as pl
from jax.experimental.pallas import tpu as pltpu

F32 = jnp.float32
BF16 = jnp.bfloat16

MLSTM_HEADS = 8
FOX_HEAD_DIM = 128
GATE_SOFT_CAP = 15.0
HEAD_NORM_EPS = 1e-6
LN_EPS = 1e-5
N_MIXERS = 2

LANES = 128
SUBLANES = 8
NEG_BIG = -1e30
LOG2E = math.log2(math.e)

BIG_TM = 2048
MM_TN = 512
FFN_TN = 256
LN_TM = 1024
LN_OUT_ROWS = 128
LN_TK = 1024
LN_H_CHUNKS = 8
LN_NCHUNK = 1024
LN_ROWS = 16
GATE_TM = 512
FFN_PAD = 1024
MLSTM_CHUNK = 256
MLSTM_HEADS_PER_STEP = 8
FOX_TQ = 512
FOX_TS = 512
FOX_HEADS_PER_STEP = 4
VMEM_LIMIT = 56 * 1024 * 1024


def _tile(n, pref):
    if n <= pref:
        return n
    t = (pref // LANES) * LANES
    while t >= LANES:
        if n % t == 0:
            return t
        t -= LANES
    raise ValueError(f"no lane-aligned tile of {n} below {pref}")


def _params(*sem):
    return pltpu.CompilerParams(dimension_semantics=sem, vmem_limit_bytes=VMEM_LIMIT)


def _dot(a, b):
    return jnp.dot(a, b, preferred_element_type=F32)


def _dot_nt(a, b):
    return lax.dot_general(a, b, (((1,), (1,)), ((), ())), preferred_element_type=F32)


def _dot_tn(a, b):
    return lax.dot_general(a, b, (((0,), (0,)), ((), ())), preferred_element_type=F32)


def _split3(x):
    hi = x.astype(BF16)
    r1 = x - hi.astype(F32)
    mid = r1.astype(BF16)
    lo = (r1 - mid.astype(F32)).astype(BF16)
    return hi, mid, lo


def _sigmoid(x):
    return 1.0 / (1.0 + jnp.exp(-x))


def _log_sigmoid(x):
    return jnp.minimum(x, 0.0) - jnp.log1p(jnp.exp(-jnp.abs(x)))


def _soft_cap(z):
    return GATE_SOFT_CAP * jnp.tanh(z / GATE_SOFT_CAP)


def _proj_kernel(x_ref, w_ref, side_ref, obf_ref, *more_out, n_scaled, scale, n_bf):
    j = pl.program_id(1)
    side_bf_ref = more_out[-1]

    def product():
        acc = _dot_nt(x_ref[...], w_ref[...].astype(BF16))
        side_bf_ref[...] = side_ref[...].astype(BF16)
        return acc * jnp.where(j < n_scaled, scale, 1.0) if n_scaled else acc

    if len(more_out) == 1:
        obf_ref[...] = product().astype(BF16)
        return
    o32_ref = more_out[0]

    @pl.when(j < n_bf)
    def _():
        obf_ref[...] = product().astype(BF16)

    @pl.when(j >= n_bf)
    def _():
        o32_ref[...] = product()


def _proj(x, wt_stack, idx, bf_cols, f32_cols, scaled_cols, scale, side_stack):
    m, k = x.shape
    tm, tn = _tile(m, BIG_TM), _tile(math.gcd(bf_cols, f32_cols), MM_TN)
    assert bf_cols % tn == 0 and f32_cols % tn == 0 and scaled_cols % tn == 0
    n_bf = bf_cols // tn
    ni, nj = m // tm, (bf_cols + f32_cols) // tn
    _, rows, cols = side_stack.shape
    sr = next(r for r in range(2 * SUBLANES, rows + 1, 2 * SUBLANES)
              if rows % r == 0 and r * ni * nj >= rows)
    nblocks = rows // sr
    block = lambda i, j: jnp.minimum(i * nj + j, nblocks - 1)
    out_specs = [pl.BlockSpec((tm, tn), lambda i, j: (i, jnp.minimum(j, n_bf - 1)))]
    out_shape = [jax.ShapeDtypeStruct((m, bf_cols), BF16)]
    if f32_cols:
        out_specs.append(pl.BlockSpec((tm, tn), lambda i, j: (i, jnp.maximum(j - n_bf, 0))))
        out_shape.append(jax.ShapeDtypeStruct((m, f32_cols), F32))
    out_specs.append(pl.BlockSpec((sr, cols), lambda i, j: (block(i, j), 0)))
    out_shape.append(jax.ShapeDtypeStruct((rows, cols), BF16))
    res = pl.pallas_call(
        functools.partial(_proj_kernel, n_scaled=scaled_cols // tn, scale=scale, n_bf=n_bf),
        grid=(ni, nj),
        in_specs=[pl.BlockSpec((tm, k), lambda i, j: (i, 0), pipeline_mode=pl.Buffered(1)),
                  pl.BlockSpec((None, tn, k), lambda i, j: (idx, j, 0)),
                  pl.BlockSpec((None, sr, cols), lambda i, j: (idx, block(i, j), 0))],
        out_specs=out_specs,
        out_shape=out_shape,
        compiler_params=_params("arbitrary", "arbitrary"),
        name="proj",
    )(x, wt_stack, side_stack)
    return (res[0], res[1], res[2]) if f32_cols else (res[0], None, res[1])


def _gate_kernel(h_ref, w_ref, b_ref, o_ref, *, n_gates):
    x = h_ref[...]
    w = w_ref[...]
    x_hi = x.astype(BF16)
    x_lo = (x - x_hi.astype(F32)).astype(BF16)
    w_hi = w.astype(BF16)
    w_lo = (w - w_hi.astype(F32)).astype(BF16)
    both = _dot_nt(x_hi, jnp.concatenate([w_hi, w_lo], axis=0))
    acc = both[:, :LANES] + both[:, LANES:] + _dot_nt(x_lo, w_hi)
    lane = lax.broadcasted_iota(jnp.int32, acc.shape, 1)
    o_ref[...] = jnp.where(lane < n_gates, acc + b_ref[...], 0.0)


def _gate_proj(h32, wt_stack, idx, col0, bias):
    m, d = h32.shape
    g = wt_stack.shape[1] - col0
    assert col0 % LANES == 0 and 0 < g <= LANES
    b_pad = jnp.pad(bias.astype(F32), (0, LANES - g)).reshape(1, LANES)
    tm = _tile(m, GATE_TM)
    return pl.pallas_call(
        functools.partial(_gate_kernel, n_gates=g),
        grid=(m // tm,),
        in_specs=[pl.BlockSpec((tm, d), lambda i: (i, 0)),
                  pl.BlockSpec((None, LANES, d), lambda i: (idx, col0 // LANES, 0)),
                  pl.BlockSpec((1, LANES), lambda i: (0, 0))],
        out_specs=pl.BlockSpec((tm, LANES), lambda i: (i, 0)),
        out_shape=jax.ShapeDtypeStruct((m, LANES), F32),
        compiler_params=_params("parallel"),
        name="gate_proj",
    )(h32, wt_stack, b_pad)


def _mm_res_ln_kernel(a_ref, b_ref, h_ref, g_ref, beta_ref, o32_ref, obf_ref,
                      acc_ref, part_scr, mu_scr, rstd_scr, *, alpha, nk, nh):
    t = pl.program_id(1)
    tm, d = acc_ref.shape
    rows_out = o32_ref.shape[0]
    nch = min(d, LN_NCHUNK)
    chunks = range(0, d, nch)

    @pl.when(t == 0)
    def _():
        a = a_ref[...]
        for n0 in chunks:
            acc_ref[:, n0:n0 + nch] = _dot(a, b_ref[:, n0:n0 + nch])

    @pl.when((t > 0) & (t < nk))
    def _():
        a = a_ref[...]
        for n0 in chunks:
            acc_ref[:, n0:n0 + nch] += _dot(a, b_ref[:, n0:n0 + nch])

    @pl.when(t < nh)
    def _():
        hc = d // nh
        cols = pl.ds(pl.multiple_of(t * hc, hc), hc)
        acc_ref[:, cols] += alpha * h_ref[...]

    @pl.when(t >= nk)
    def _():
        rb = min(rows_out, LN_ROWS)
        row0 = (t - nk) * rows_out
        cols = [slice(c, c + LANES) for c in range(0, d, LANES)]

        def row_groups(fn):
            def body(r, carry):
                local = pl.multiple_of(r * rb, rb)
                fn(pl.ds(local, rb), pl.ds(pl.multiple_of(row0 + local, rb), rb))
                return carry
            lax.fori_loop(0, rows_out // rb, body, 0)

        def lane_mean(part):
            total = jnp.sum(part, axis=1, keepdims=True) * (1.0 / d)
            return jnp.broadcast_to(total, part.shape)

        def tree_sum(xs):
            while len(xs) > 1:
                xs = [xs[n] + xs[n + 1] for n in range(0, len(xs) - 1, 2)] + xs[len(xs) & ~1:]
            return xs[0]

        def pass_sum(rows, src):
            part_scr[rows, :] = tree_sum([acc_ref[src, c] for c in cols])

        def pass_sq(rows, src):
            mu = mu_scr[rows, :]
            dev = [acc_ref[src, c] - mu for c in cols]
            part_scr[rows, :] = tree_sum([x * x for x in dev])

        def pass_norm(rows, src):
            mu = mu_scr[rows, :]
            rstd = rstd_scr[rows, :]
            for c in cols:
                out = (acc_ref[src, c] - mu) * rstd * g_ref[:, c] + beta_ref[:, c]
                o32_ref[rows, c] = out
                obf_ref[rows, c] = out.astype(BF16)

        row_groups(pass_sum)
        mu_scr[...] = lane_mean(part_scr[...])
        row_groups(pass_sq)
        rstd_scr[...] = lax.rsqrt(lane_mean(part_scr[...]) + LN_EPS)
        row_groups(pass_norm)


def _mm_res_ln(a, b, h32, g, beta, alpha):
    m, kdim = a.shape
    d = b.shape[1]
    tm, tk = _tile(m, LN_TM), _tile(kdim, LN_TK)
    nk = kdim // tk
    nh = max(n for n in range(1, min(nk, LN_H_CHUNKS) + 1) if d % (n * LANES) == 0)
    rows_out = min(tm, LN_OUT_ROWS)
    ns = tm // rows_out
    ni = m // tm
    sweep = lambda t: t >= nk
    i_of = lambda i, t: jnp.where(sweep(t), jnp.minimum(i + 1, ni - 1), i)
    k_of = lambda t: jnp.where(sweep(t), 0, t)
    out_row = lambda i, t: (i * ns + jnp.maximum(t - nk, 0), 0)
    return pl.pallas_call(
        functools.partial(_mm_res_ln_kernel, alpha=alpha, nk=nk, nh=nh),
        grid=(ni, nk + ns),
        in_specs=[pl.BlockSpec((tm, tk), lambda i, t: (i_of(i, t), k_of(t))),
                  pl.BlockSpec((tk, d), lambda i, t: (k_of(t), 0)),
                  pl.BlockSpec((tm, d // nh), lambda i, t: (i_of(i, t), jnp.minimum(k_of(t), nh - 1))),
                  pl.BlockSpec((1, d), lambda i, t: (0, 0)),
                  pl.BlockSpec((1, d), lambda i, t: (0, 0))],
        out_specs=[pl.BlockSpec((rows_out, d), out_row),
                   pl.BlockSpec((rows_out, d), out_row)],
        out_shape=[jax.ShapeDtypeStruct((m, d), F32), jax.ShapeDtypeStruct((m, d), BF16)],
        scratch_shapes=[pltpu.VMEM((tm, d), F32)] + [pltpu.VMEM((rows_out, LANES), F32)] * 3,
        compiler_params=_params("parallel", "arbitrary"),
        name="proj_res_ln",
    )(a, b, h32, g.reshape(1, d).astype(F32), beta.reshape(1, d).astype(F32))


def _ffn_up_kernel(x_ref, wg_ref, wu_ref, wd_ref, o_ref, wd_bf_ref, *, n_real, nj, wd_blocks):
    i = pl.program_id(0)
    j = pl.program_id(1)

    def cast_down_rows():
        real_rows = i * nj + j < wd_blocks
        wd_bf_ref[...] = jnp.where(real_rows, wd_ref[...], 0.0).astype(BF16)

    @pl.when(j < n_real)
    def _():
        x = x_ref[...]
        gate = _dot(x, wg_ref[...].astype(BF16))
        up = _dot(x, wu_ref[...].astype(BF16))
        o_ref[...] = (gate * _sigmoid(gate) * up).astype(o_ref.dtype)
        cast_down_rows()

    @pl.when(j >= n_real)
    def _():
        o_ref[...] = jnp.zeros_like(o_ref)
        cast_down_rows()


def _ffn_up(x, w_up_stack, w_down_stack, idx, ffn_pad):
    m, d = x.shape
    f = w_up_stack.shape[2] // 2
    d_out = w_down_stack.shape[2]
    tm, tn = _tile(m, BIG_TM), _tile(f, FFN_TN)
    assert ffn_pad % tn == 0
    n_real = f // tn
    ni, nj = m // tm, ffn_pad // tn
    wd_rows = ffn_pad // (ni * nj)
    assert wd_rows * ni * nj == ffn_pad and wd_rows % (2 * SUBLANES) == 0 and f % wd_rows == 0
    wd_blocks = f // wd_rows
    real = lambda j: jnp.minimum(j, n_real - 1)
    return pl.pallas_call(
        functools.partial(_ffn_up_kernel, n_real=n_real, nj=nj, wd_blocks=wd_blocks),
        grid=(ni, nj),
        in_specs=[pl.BlockSpec((tm, d), lambda i, j: (i, 0), pipeline_mode=pl.Buffered(1)),
                  pl.BlockSpec((None, d, tn), lambda i, j: (idx, 0, real(j))),
                  pl.BlockSpec((None, d, tn), lambda i, j: (idx, 0, n_real + real(j))),
                  pl.BlockSpec((None, wd_rows, d_out),
                               lambda i, j: (idx, jnp.minimum(i * nj + j, wd_blocks - 1), 0))],
        out_specs=[pl.BlockSpec((tm, tn), lambda i, j: (i, j)),
                   pl.BlockSpec((wd_rows, d_out), lambda i, j: (i * nj + j, 0))],
        out_shape=[jax.ShapeDtypeStruct((m, ffn_pad), BF16),
                   jax.ShapeDtypeStruct((ffn_pad, d_out), BF16)],
        compiler_params=_params("arbitrary", "arbitrary"),
        name="ffn_up",
    )(x, w_up_stack, w_up_stack, w_down_stack)


def _mlstm_kernel(q_ref, k_ref, v_ref, o_ref, gc_ref, gri_ref, grf_ref, ng_ref, out_ref,
                  ct_ref, m_ref, *, chunk, dk, dv, heads, hps):
    hp = pl.program_id(1)
    c = pl.program_id(2)
    L = chunk

    @pl.when(c == 0)
    def _():
        ct_ref[...] = jnp.zeros_like(ct_ref)
        m_ref[...] = jnp.zeros_like(m_ref)

    gc = gc_ref[...]
    lane = lax.broadcasted_iota(jnp.int32, (L, LANES), 1)
    ti = lax.broadcasted_iota(jnp.int32, (L, L), 0)
    si = lax.broadcasted_iota(jnp.int32, (L, L), 1)
    causal = si <= ti
    tri_lo = causal.astype(BF16)
    tri_up = (ti <= si).astype(BF16)
    one_lane = (lane == 0).astype(F32)

    for hl in range(hps):
        h = hp * hps + hl
        kcols = slice(hl * dk, (hl + 1) * dk)
        vcols = slice(hl * dv, (hl + 1) * dv)

        ig_c = jnp.sum(jnp.where(lane == h, gc, 0.0), axis=1, keepdims=True)
        fg_c = jnp.sum(jnp.where(lane == heads + h, gc, 0.0), axis=1, keepdims=True)
        ig_r = gri_ref[pl.ds(h, 1), :]
        fg_r = grf_ref[pl.ds(h, 1), :]
        ig_c, ig_r = _soft_cap(ig_c), _soft_cap(ig_r)
        lf_c, lf_r = _log_sigmoid(_soft_cap(fg_c)), _log_sigmoid(_soft_cap(fg_r))

        b_c = sum(_dot(tri_lo, p) for p in _split3(jnp.broadcast_to(lf_c, (L, LANES))))[:, :1]
        b_r = sum(_dot(p, tri_up) for p in _split3(jnp.broadcast_to(lf_r, (SUBLANES, L))))[:1, :]

        m_prev = m_ref[hl]
        d_log = jnp.where(causal, b_c - b_r + ig_r, NEG_BIG)
        g_log = b_c + m_prev
        m_t = jnp.maximum(g_log, jnp.max(d_log, axis=1, keepdims=True))
        w_intra = jnp.exp(d_log - m_t)
        w_inter = jnp.exp(g_log - m_t)

        q = q_ref[:, kcols]
        k = k_ref[:, kcols]
        v = v_ref[:, vcols]
        s = _dot_nt(q, k) * w_intra
        ct = ct_ref[hl]
        inter = _dot(q, ct.astype(BF16))
        num = _dot(s.astype(BF16), v) + w_inter * inter[:, :dv]
        den = jnp.sum(s, axis=1, keepdims=True) + w_inter * inter[:, dv:dv + 1]
        hh = num * (1.0 / jnp.maximum(jnp.abs(den), jnp.exp(-m_t)))

        b_end = b_c[L - 1:L, :]
        a_c = b_end - b_c + ig_c
        a_r = b_end - b_r + ig_r
        m_new = jnp.maximum(b_end + m_prev, jnp.max(a_r, axis=1, keepdims=True))
        decay = jnp.exp(b_end + m_prev - m_new)
        wa_c = jnp.exp(a_c - m_new)
        wv = jnp.concatenate([wa_c * v.astype(F32), wa_c * one_lane], axis=1).astype(BF16)
        ct_ref[hl] = decay * ct + _dot_tn(k, wv)
        m_ref[hl] = m_new

        hn = hh * lax.rsqrt(jnp.mean(hh * hh, axis=1, keepdims=True) + HEAD_NORM_EPS)
        out_ref[:, vcols] = (hn * ng_ref[:, vcols] * _sigmoid(o_ref[:, vcols])).astype(out_ref.dtype)


def _mlstm_core(qkv, o, gates, gates_t, norm_g, batch, seq):
    heads = MLSTM_HEADS
    assert heads == SUBLANES, "row-form gate blocks assume one sublane tile per gate kind"
    m = qkv.shape[0]
    hv = o.shape[1]
    dv = hv // heads
    dk = (qkv.shape[1] - hv) // (2 * heads)
    assert 2 * heads * dk == hv, "v blocks are addressed in units of dv"
    hps = MLSTM_HEADS_PER_STEP
    assert heads % hps == 0
    groups = heads // hps
    chunk = _tile(seq, MLSTM_CHUNK)
    nc = seq // chunk
    row = lambda b, h, c: b * nc + c
    return pl.pallas_call(
        functools.partial(_mlstm_kernel, chunk=chunk, dk=dk, dv=dv, heads=heads, hps=hps),
        grid=(batch, groups, nc),
        in_specs=[pl.BlockSpec((chunk, hps * dk), lambda b, h, c: (row(b, h, c), h)),
                  pl.BlockSpec((chunk, hps * dk), lambda b, h, c: (row(b, h, c), groups + h)),
                  pl.BlockSpec((chunk, hps * dv), lambda b, h, c: (row(b, h, c), groups + h)),
                  pl.BlockSpec((chunk, hps * dv), lambda b, h, c: (row(b, h, c), h)),
                  pl.BlockSpec((chunk, LANES), lambda b, h, c: (row(b, h, c), 0)),
                  pl.BlockSpec((SUBLANES, chunk), lambda b, h, c: (0, row(b, h, c))),
                  pl.BlockSpec((SUBLANES, chunk), lambda b, h, c: (1, row(b, h, c))),
                  pl.BlockSpec((1, hps * dv), lambda b, h, c: (0, h))],
        out_specs=pl.BlockSpec((chunk, hps * dv), lambda b, h, c: (row(b, h, c), h)),
        out_shape=jax.ShapeDtypeStruct((m, hv), BF16),
        scratch_shapes=[pltpu.VMEM((hps, dk, dv + LANES), F32), pltpu.VMEM((hps, 1, 1), F32)],
        compiler_params=_params("parallel", "parallel", "arbitrary"),
        name="mlstm_core",
    )(qkv, qkv, qkv, o, gates, gates_t, gates_t, norm_g.reshape(1, hv).astype(F32))


def _fox_cum_kernel(g_ref, cum_ref, carry_ref, *, ts):
    @pl.when(pl.program_id(1) == 0)
    def _():
        carry_ref[...] = jnp.zeros_like(carry_ref)

    logf = _log_sigmoid(g_ref[...])
    ti = lax.broadcasted_iota(jnp.int32, (ts, ts), 0)
    si = lax.broadcasted_iota(jnp.int32, (ts, ts), 1)
    tri_lo = (si <= ti).astype(BF16)
    cum = sum(_dot(tri_lo, p) for p in _split3(logf)) + carry_ref[...]
    carry_ref[...] = cum[ts - 1:ts, :]
    cum_ref[...] = cum


def _fox_cumsum(gates, batch, seq):
    ts = _tile(seq, FOX_TS)
    ns = seq // ts
    return pl.pallas_call(
        functools.partial(_fox_cum_kernel, ts=ts),
        grid=(batch, ns),
        in_specs=[pl.BlockSpec((ts, LANES), lambda b, s: (b * ns + s, 0))],
        out_specs=pl.BlockSpec((ts, LANES), lambda b, s: (b * ns + s, 0)),
        out_shape=jax.ShapeDtypeStruct((batch * seq, LANES), F32),
        scratch_shapes=[pltpu.VMEM((1, LANES), F32)],
        compiler_params=_params("parallel", "arbitrary"),
        name="fox_cumsum",
    )(gates)


def _bias_lanes(col, own_first):
    n = col.shape[0]
    hi, mid, lo = (p.astype(F32) for p in _split3(col))
    lane = lax.broadcasted_iota(jnp.int32, (n, LANES), 1)
    base = 0 if own_first else 3
    ones = ((lane >= 3 - base) & (lane < 6 - base)).astype(F32)
    out = jnp.where(lane == base, hi, jnp.where(lane == base + 1, mid, jnp.where(lane == base + 2, lo, ones)))
    return out.astype(BF16)


def _fox_attn_kernel(q_ref, k_ref, v_ref, cc_ref, o_ref, ka_scr, vt_scr, s_scr, p_scr,
                     a_scr, m_scr, l_scr, acc_scr, *, tq, seq, hps):
    hp = pl.program_id(1)
    i = pl.program_id(2)
    dh = LANES
    local_heads = range(hps)

    def head_col(rows, hl):
        lane = lax.broadcasted_iota(jnp.int32, rows.shape, 1)
        return jnp.sum(jnp.where(lane == hp * hps + hl, rows, 0.0), axis=1, keepdims=True) * LOG2E

    @pl.when(i == 0)
    def _():
        for hl in local_heads:
            hc = slice(hl * dh, (hl + 1) * dh)
            for r0 in range(0, seq, tq):
                ka_scr[hl, r0:r0 + tq, :LANES] = k_ref[r0:r0 + tq, hc]
                ka_scr[hl, r0:r0 + tq, LANES:] = _bias_lanes(-head_col(cc_ref[r0:r0 + tq, :], hl), True)
                vt_scr[hl, :, r0:r0 + tq] = v_ref[r0:r0 + tq, hc].astype(F32).T.astype(BF16)

    q0 = pl.multiple_of(i * tq, tq)
    cum_q = cc_ref[pl.ds(q0, tq), :]
    qa = [jnp.concatenate([q_ref[:, hl * dh:(hl + 1) * dh], _bias_lanes(head_col(cum_q, hl), False)], axis=1)
          for hl in local_heads]

    m_scr[...] = jnp.full_like(m_scr, NEG_BIG)
    l_scr[...] = jnp.zeros_like(l_scr)
    acc_scr[...] = jnp.zeros_like(acc_scr)
    a_scr[...] = jnp.ones_like(a_scr)
    for hl in local_heads:
        p_scr[hl, 1] = jnp.zeros((tq, tq), BF16)

    def logits(j, parity):
        start = pl.multiple_of(j * tq, tq)
        for hl in local_heads:
            s_scr[hl, parity] = _dot_nt(ka_scr[hl, pl.ds(start, tq), :], qa[hl])

    def softmax(parity, on_diagonal):
        for hl in local_heads:
            st = s_scr[hl, parity]
            if on_diagonal:
                ki = lax.broadcasted_iota(jnp.int32, (tq, tq), 0)
                qi = lax.broadcasted_iota(jnp.int32, (tq, tq), 1)
                st = jnp.where(ki <= qi, st, NEG_BIG)
            m_old = m_scr[hl]
            m_new = jnp.maximum(m_old, jnp.max(st, axis=0, keepdims=True))
            p = jnp.exp2(st - m_new)
            a = jnp.exp2(m_old - m_new)
            l_scr[hl] = a * l_scr[hl] + jnp.sum(p, axis=0, keepdims=True)
            m_scr[hl] = m_new
            p_scr[hl, parity] = p.astype(BF16)
            a_scr[hl] = a

    def values(j, parity):
        start = pl.multiple_of(jnp.maximum(j, 0) * tq, tq)
        for hl in local_heads:
            acc_scr[hl] = a_scr[hl] * acc_scr[hl] + _dot(vt_scr[hl, :, pl.ds(start, tq)], p_scr[hl, parity])

    def step(j, parity):
        values(j - 1, 1 - parity)
        logits(j + 1, 1 - parity)
        softmax(parity, False)

    def finish(parity):
        values(i - 1, 1 - parity)
        softmax(parity, True)
        values(i, parity)

    logits(0, 0)

    def pair(t, carry):
        step(2 * t, 0)
        step(2 * t + 1, 1)
        return carry

    lax.fori_loop(0, i // 2, pair, 0)

    @pl.when(i % 2 == 0)
    def _():
        finish(0)

    @pl.when(i % 2 == 1)
    def _():
        step(i - 1, 0)
        finish(1)

    for hl in local_heads:
        o_ref[:, hl * dh:(hl + 1) * dh] = (acc_scr[hl] * (1.0 / l_scr[hl])).T.astype(o_ref.dtype)


def _fox_attn(qkv, cum, batch, seq):
    dh = FOX_HEAD_DIM
    assert dh == LANES
    m = qkv.shape[0]
    heads = qkv.shape[1] // (3 * dh)
    assert heads <= LANES
    hps = FOX_HEADS_PER_STEP if heads % FOX_HEADS_PER_STEP == 0 else 1
    groups = heads // hps
    tq = _tile(seq, FOX_TQ)
    nq = seq // tq
    return pl.pallas_call(
        functools.partial(_fox_attn_kernel, tq=tq, seq=seq, hps=hps),
        grid=(batch, groups, nq),
        in_specs=[pl.BlockSpec((tq, hps * dh), lambda b, h, i: (b * nq + i, h)),
                  pl.BlockSpec((seq, hps * dh), lambda b, h, i: (b, groups + h)),
                  pl.BlockSpec((seq, hps * dh), lambda b, h, i: (b, 2 * groups + h)),
                  pl.BlockSpec((seq, LANES), lambda b, h, i: (b, 0))],
        out_specs=pl.BlockSpec((tq, hps * dh), lambda b, h, i: (b * nq + i, h)),
        out_shape=jax.ShapeDtypeStruct((m, heads * dh), BF16),
        scratch_shapes=[pltpu.VMEM((hps, seq, 2 * LANES), BF16), pltpu.VMEM((hps, dh, seq), BF16),
                        pltpu.VMEM((hps, 2, tq, tq), F32), pltpu.VMEM((hps, 2, tq, tq), BF16),
                        pltpu.VMEM((hps, 1, tq), F32), pltpu.VMEM((hps, 1, tq), F32),
                        pltpu.VMEM((hps, 1, tq), F32), pltpu.VMEM((hps, dh, tq), F32)],
        compiler_params=_params("parallel", "parallel", "arbitrary"),
        name="fox_attn",
    )(qkv, qkv, qkv, cum)


def kernel(x, mlstm_w_in, mlstm_b_gate, mlstm_norm_g, mlstm_w_out, fox_w_in, fox_b_f, fox_w_out,
           ln_mix_g, ln_mix_b, ffn_w_up, ffn_w_down, ln_ffn_g, ln_ffn_b):
    batch, seq, d = x.shape
    depth = ln_mix_g.shape[0]
    alpha = (2 * depth) ** 0.25
    m = batch * seq
    ffn = ffn_w_down.shape[1]
    ffn_pad = -(-ffn // FFN_PAD) * FFN_PAD
    hv = mlstm_w_out.shape[1]
    mlstm_main = mlstm_w_in.shape[2] - 2 * MLSTM_HEADS
    mlstm_qk = (mlstm_main - 2 * hv) // 2
    mlstm_dk = mlstm_qk // MLSTM_HEADS
    fox_w = fox_w_out.shape[1]

    mlstm_wt = jnp.swapaxes(mlstm_w_in, 1, 2)
    fox_wt = jnp.swapaxes(fox_w_in, 1, 2)

    h32 = x.reshape(m, d)
    hbf = h32.astype(BF16)
    for layer in range(depth):
        slot = layer // N_MIXERS
        if layer % N_MIXERS == 0:
            qkv, o, w_out = _proj(hbf, mlstm_wt, slot, mlstm_main - hv, hv, mlstm_qk, mlstm_dk ** -0.5,
                                  mlstm_w_out)
            gates = _gate_proj(h32, mlstm_wt, slot, mlstm_main, mlstm_b_gate[slot])
            mixed = _mlstm_core(qkv, o, gates, gates.T, mlstm_norm_g[slot], batch, seq)
        else:
            qkv, _, w_out = _proj(hbf, fox_wt, slot, 3 * fox_w, 0, fox_w, FOX_HEAD_DIM ** -0.5 * LOG2E,
                                  fox_w_out)
            gates = _gate_proj(h32, fox_wt, slot, 3 * fox_w, fox_b_f[slot])
            mixed = _fox_attn(qkv, _fox_cumsum(gates, batch, seq), batch, seq)
        h32, hbf = _mm_res_ln(mixed, w_out, h32, ln_mix_g[layer], ln_mix_b[layer], alpha)

        hidden, w_down = _ffn_up(hbf, ffn_w_up, ffn_w_down, layer, ffn_pad)
        h32, hbf = _mm_res_ln(hidden, w_down, h32, ln_ffn_g[layer], ln_ffn_b[layer], alpha)
    return h32.reshape(batch, seq, d)
```

```python
import functools
import math

import jax
import jax.numpy as jnp
from jax import lax
from jax.experimental import pallas as pl
from jax.experimental.pallas import tpu as pltpu

F32 = jnp.float32
BF16 = jnp.bfloat16

MLSTM_HEADS = 8
FOX_HEAD_DIM = 128
GATE_SOFT_CAP = 15.0
HEAD_NORM_EPS = 1e-6
LN_EPS = 1e-5
N_MIXERS = 2

LANES = 128
SUBLANES = 8
NEG_BIG = -1e30
LOG2E = math.log2(math.e)

BIG_TM = 2048
MM_TN = 512
FFN_TN = 256
LN_TM = 1024
LN_OUT_ROWS = 128
LN_TK = 1024
LN_H_CHUNKS = 8
LN_NCHUNK = 1024
LN_ROWS = 16
GATE_TM = 512
FFN_PAD = 1024
MLSTM_CHUNK = 256
MLSTM_HEADS_PER_STEP = 8
FOX_TQ = 512
FOX_TS = 512
FOX_HEADS_PER_STEP = 4
VMEM_LIMIT = 56 * 1024 * 1024


def _tile(n, pref):
    if n <= pref:
        return n
    t = (pref // LANES) * LANES
    while t >= LANES:
        if n % t == 0:
            return t
        t -= LANES
    raise ValueError(f"no lane-aligned tile of {n} below {pref}")


def _params(*sem):
    return pltpu.CompilerParams(dimension_semantics=sem, vmem_limit_bytes=VMEM_LIMIT)


def _dot(a, b):
    return jnp.dot(a, b, preferred_element_type=F32)


def _dot_nt(a, b):
    return lax.dot_general(a, b, (((1,), (1,)), ((), ())), preferred_element_type=F32)


def _dot_tn(a, b):
    return lax.dot_general(a, b, (((0,), (0,)), ((), ())), preferred_element_type=F32)


def _split3(x):
    hi = x.astype(BF16)
    r1 = x - hi.astype(F32)
    mid = r1.astype(BF16)
    lo = (r1 - mid.astype(F32)).astype(BF16)
    return hi, mid, lo


def _sigmoid(x):
    return 1.0 / (1.0 + jnp.exp(-x))


def _log_sigmoid(x):
    return jnp.minimum(x, 0.0) - jnp.log1p(jnp.exp(-jnp.abs(x)))


def _soft_cap(z):
    return GATE_SOFT_CAP * jnp.tanh(z / GATE_SOFT_CAP)


def _proj_kernel(x_ref, w_ref, side_ref, obf_ref, *more_out, n_scaled, scale, n_bf):
    j = pl.program_id(1)
    side_bf_ref = more_out[-1]

    def product():
        acc = _dot_nt(x_ref[...], w_ref[...].astype(BF16))
        side_bf_ref[...] = side_ref[...].astype(BF16)
        return acc * jnp.where(j < n_scaled, scale, 1.0) if n_scaled else acc

    if len(more_out) == 1:
        obf_ref[...] = product().astype(BF16)
        return
    o32_ref = more_out[0]

    @pl.when(j < n_bf)
    def _():
        obf_ref[...] = product().astype(BF16)

    @pl.when(j >= n_bf)
    def _():
        o32_ref[...] = product()


def _proj(x, wt_stack, idx, bf_cols, f32_cols, scaled_cols, scale, side_stack):
    m, k = x.shape
    tm, tn = _tile(m, BIG_TM), _tile(math.gcd(bf_cols, f32_cols), MM_TN)
    assert bf_cols % tn == 0 and f32_cols % tn == 0 and scaled_cols % tn == 0
    n_bf = bf_cols // tn
    ni, nj = m // tm, (bf_cols + f32_cols) // tn
    _, rows, cols = side_stack.shape
    sr = next(r for r in range(2 * SUBLANES, rows + 1, 2 * SUBLANES)
              if rows % r == 0 and r * ni * nj >= rows)
    nblocks = rows // sr
    block = lambda i, j: jnp.minimum(i * nj + j, nblocks - 1)
    out_specs = [pl.BlockSpec((tm, tn), lambda i, j: (i, jnp.minimum(j, n_bf - 1)))]
    out_shape = [jax.ShapeDtypeStruct((m, bf_cols), BF16)]
    if f32_cols:
        out_specs.append(pl.BlockSpec((tm, tn), lambda i, j: (i, jnp.maximum(j - n_bf, 0))))
        out_shape.append(jax.ShapeDtypeStruct((m, f32_cols), F32))
    out_specs.append(pl.BlockSpec((sr, cols), lambda i, j: (block(i, j), 0)))
    out_shape.append(jax.ShapeDtypeStruct((rows, cols), BF16))
    res = pl.pallas_call(
        functools.partial(_proj_kernel, n_scaled=scaled_cols // tn, scale=scale, n_bf=n_bf),
        grid=(ni, nj),
        in_specs=[pl.BlockSpec((tm, k), lambda i, j: (i, 0), pipeline_mode=pl.Buffered(1)),
                  pl.BlockSpec((None, tn, k), lambda i, j: (idx, j, 0)),
                  pl.BlockSpec((None, sr, cols), lambda i, j: (idx, block(i, j), 0))],
        out_specs=out_specs,
        out_shape=out_shape,
        compiler_params=_params("arbitrary", "arbitrary"),
        name="proj",
    )(x, wt_stack, side_stack)
    return (res[0], res[1], res[2]) if f32_cols else (res[0], None, res[1])


def _gate_kernel(h_ref, w_ref, b_ref, o_ref, *, n_gates):
    x = h_ref[...]
    w = w_ref[...]
    x_hi = x.astype(BF16)
    x_lo = (x - x_hi.astype(F32)).astype(BF16)
    w_hi = w.astype(BF16)
    w_lo = (w - w_hi.astype(F32)).astype(BF16)
    both = _dot_nt(x_hi, jnp.concatenate([w_hi, w_lo], axis=0))
    acc = both[:, :LANES] + both[:, LANES:] + _dot_nt(x_lo, w_hi)
    lane = lax.broadcasted_iota(jnp.int32, acc.shape, 1)
    o_ref[...] = jnp.where(lane < n_gates, acc + b_ref[...], 0.0)


def _gate_proj(h32, wt_stack, idx, col0, bias):
    m, d = h32.shape
    g = wt_stack.shape[1] - col0
    assert col0 % LANES == 0 and 0 < g <= LANES
    b_pad = jnp.pad(bias.astype(F32), (0, LANES - g)).reshape(1, LANES)
    tm = _tile(m, GATE_TM)
    return pl.pallas_call(
        functools.partial(_gate_kernel, n_gates=g),
        grid=(m // tm,),
        in_specs=[pl.BlockSpec((tm, d), lambda i: (i, 0)),
                  pl.BlockSpec((None, LANES, d), lambda i: (idx, col0 // LANES, 0)),
                  pl.BlockSpec((1, LANES), lambda i: (0, 0))],
        out_specs=pl.BlockSpec((tm, LANES), lambda i: (i, 0)),
        out_shape=jax.ShapeDtypeStruct((m, LANES), F32),
        compiler_params=_params("parallel"),
        name="gate_proj",
    )(h32, wt_stack, b_pad)


def _mm_res_ln_kernel(a_ref, b_ref, h_ref, g_ref, beta_ref, o32_ref, obf_ref,
                      acc_ref, part_scr, mu_scr, rstd_scr, *, alpha, nk, nh):
    t = pl.program_id(1)
    tm, d = acc_ref.shape
    rows_out = o32_ref.shape[0]
    nch = min(d, LN_NCHUNK)
    chunks = range(0, d, nch)

    @pl.when(t == 0)
    def _():
        a = a_ref[...]
        for n0 in chunks:
            acc_ref[:, n0:n0 + nch] = _dot(a, b_ref[:, n0:n0 + nch])

    @pl.when((t > 0) & (t < nk))
    def _():
        a = a_ref[...]
        for n0 in chunks:
            acc_ref[:, n0:n0 + nch] += _dot(a, b_ref[:, n0:n0 + nch])

    @pl.when(t < nh)
    def _():
        hc = d // nh
        cols = pl.ds(pl.multiple_of(t * hc, hc), hc)
        acc_ref[:, cols] += alpha * h_ref[...]

    @pl.when(t >= nk)
    def _():
        rb = min(rows_out, LN_ROWS)
        row0 = (t - nk) * rows_out
        cols = [slice(c, c + LANES) for c in range(0, d, LANES)]

        def row_groups(fn):
            def body(r, carry):
                local = pl.multiple_of(r * rb, rb)
                fn(pl.ds(local, rb), pl.ds(pl.multiple_of(row0 + local, rb), rb))
                return carry
            lax.fori_loop(0, rows_out // rb, body, 0)

        def lane_mean(part):
            total = jnp.sum(part, axis=1, keepdims=True) * (1.0 / d)
            return jnp.broadcast_to(total, part.shape)

        def tree_sum(xs):
            while len(xs) > 1:
                xs = [xs[n] + xs[n + 1] for n in range(0, len(xs) - 1, 2)] + xs[len(xs) & ~1:]
            return xs[0]

        def pass_sum(rows, src):
            part_scr[rows, :] = tree_sum([acc_ref[src, c] for c in cols])

        def pass_sq(rows, src):
            mu = mu_scr[rows, :]
            dev = [acc_ref[src, c] - mu for c in cols]
            part_scr[rows, :] = tree_sum([x * x for x in dev])

        def pass_norm(rows, src):
            mu = mu_scr[rows, :]
            rstd = rstd_scr[rows, :]
            for c in cols:
                out = (acc_ref[src, c] - mu) * rstd * g_ref[:, c] + beta_ref[:, c]
                o32_ref[rows, c] = out
                obf_ref[rows, c] = out.astype(BF16)

        row_groups(pass_sum)
        mu_scr[...] = lane_mean(part_scr[...])
        row_groups(pass_sq)
        rstd_scr[...] = lax.rsqrt(lane_mean(part_scr[...]) + LN_EPS)
        row_groups(pass_norm)


def _mm_res_ln(a, b, h32, g, beta, alpha):
    m, kdim = a.shape
    d = b.shape[1]
    tm, tk = _tile(m, LN_TM), _tile(kdim, LN_TK)
    nk = kdim // tk
    nh = max(n for n in range(1, min(nk, LN_H_CHUNKS) + 1) if d % (n * LANES) == 0)
    rows_out = min(tm, LN_OUT_ROWS)
    ns = tm // rows_out
    ni = m // tm
    sweep = lambda t: t >= nk
    i_of = lambda i, t: jnp.where(sweep(t), jnp.minimum(i + 1, ni - 1), i)
    k_of = lambda t: jnp.where(sweep(t), 0, t)
    out_row = lambda i, t: (i * ns + jnp.maximum(t - nk, 0), 0)
    return pl.pallas_call(
        functools.partial(_mm_res_ln_kernel, alpha=alpha, nk=nk, nh=nh),
        grid=(ni, nk + ns),
        in_specs=[pl.BlockSpec((tm, tk), lambda i, t: (i_of(i, t), k_of(t))),
                  pl.BlockSpec((tk, d), lambda i, t: (k_of(t), 0)),
                  pl.BlockSpec((tm, d // nh), lambda i, t: (i_of(i, t), jnp.minimum(k_of(t), nh - 1))),
                  pl.BlockSpec((1, d), lambda i, t: (0, 0)),
                  pl.BlockSpec((1, d), lambda i, t: (0, 0))],
        out_specs=[pl.BlockSpec((rows_out, d), out_row),
                   pl.BlockSpec((rows_out, d), out_row)],
        out_shape=[jax.ShapeDtypeStruct((m, d), F32), jax.ShapeDtypeStruct((m, d), BF16)],
        scratch_shapes=[pltpu.VMEM((tm, d), F32)] + [pltpu.VMEM((rows_out, LANES), F32)] * 3,
        compiler_params=_params("parallel", "arbitrary"),
        name="proj_res_ln",
    )(a, b, h32, g.reshape(1, d).astype(F32), beta.reshape(1, d).astype(F32))


def _ffn_up_kernel(x_ref, wg_ref, wu_ref, wd_ref, o_ref, wd_bf_ref, *, n_real, nj, wd_blocks):
    i = pl.program_id(0)
    j = pl.program_id(1)

    def cast_down_rows():
        real_rows = i * nj + j < wd_blocks
        wd_bf_ref[...] = jnp.where(real_rows, wd_ref[...], 0.0).astype(BF16)

    @pl.when(j < n_real)
    def _():
        x = x_ref[...]
        gate = _dot(x, wg_ref[...].astype(BF16))
        up = _dot(x, wu_ref[...].astype(BF16))
        o_ref[...] = (gate * _sigmoid(gate) * up).astype(o_ref.dtype)
        cast_down_rows()

    @pl.when(j >= n_real)
    def _():
        o_ref[...] = jnp.zeros_like(o_ref)
        cast_down_rows()


def _ffn_up(x, w_up_stack, w_down_stack, idx, ffn_pad):
    m, d = x.shape
    f = w_up_stack.shape[2] // 2
    d_out = w_down_stack.shape[2]
    tm, tn = _tile(m, BIG_TM), _tile(f, FFN_TN)
    assert ffn_pad % tn == 0
    n_real = f // tn
    ni, nj = m // tm, ffn_pad // tn
    wd_rows = ffn_pad // (ni * nj)
    assert wd_rows * ni * nj == ffn_pad and wd_rows % (2 * SUBLANES) == 0 and f % wd_rows == 0
    wd_blocks = f // wd_rows
    real = lambda j: jnp.minimum(j, n_real - 1)
    return pl.pallas_call(
        functools.partial(_ffn_up_kernel, n_real=n_real, nj=nj, wd_blocks=wd_blocks),
        grid=(ni, nj),
        in_specs=[pl.BlockSpec((tm, d), lambda i, j: (i, 0), pipeline_mode=pl.Buffered(1)),
                  pl.BlockSpec((None, d, tn), lambda i, j: (idx, 0, real(j))),
                  pl.BlockSpec((None, d, tn), lambda i, j: (idx, 0, n_real + real(j))),
                  pl.BlockSpec((None, wd_rows, d_out),
                               lambda i, j: (idx, jnp.minimum(i * nj + j, wd_blocks - 1), 0))],
        out_specs=[pl.BlockSpec((tm, tn), lambda i, j: (i, j)),
                   pl.BlockSpec((wd_rows, d_out), lambda i, j: (i * nj + j, 0))],
        out_shape=[jax.ShapeDtypeStruct((m, ffn_pad), BF16),
                   jax.ShapeDtypeStruct((ffn_pad, d_out), BF16)],
        compiler_params=_params("arbitrary", "arbitrary"),
        name="ffn_up",
    )(x, w_up_stack, w_up_stack, w_down_stack)


def _mlstm_kernel(q_ref, k_ref, v_ref, o_ref, gc_ref, gri_ref, grf_ref, ng_ref, out_ref,
                  ct_ref, m_ref, *, chunk, dk, dv, heads, hps):
    hp = pl.program_id(1)
    c = pl.program_id(2)
    L = chunk

    @pl.when(c == 0)
    def _():
        ct_ref[...] = jnp.zeros_like(ct_ref)
        m_ref[...] = jnp.zeros_like(m_ref)

    lane = lax.broadcasted_iota(jnp.int32, (L, LANES), 1)
    sub = lax.broadcasted_iota(jnp.int32, (SUBLANES, L), 0)
    ti = lax.broadcasted_iota(jnp.int32, (L, L), 0)
    si = lax.broadcasted_iota(jnp.int32, (L, L), 1)
    causal = si <= ti
    tri_lo = causal.astype(BF16)
    tri_up = (ti <= si).astype(BF16)
    one_lane = (lane == 0).astype(F32)

    gates_cap = _soft_cap(gc_ref[...])
    cum_cols = sum(_dot(tri_lo, p) for p in _split3(_log_sigmoid(gates_cap)))
    ig_rows = _soft_cap(gri_ref[...])
    cum_rows = sum(_dot(p, tri_up) for p in _split3(_log_sigmoid(_soft_cap(grf_ref[...]))))

    for hl in range(hps):
        h = hp * hps + hl
        kcols = slice(hl * dk, (hl + 1) * dk)
        vcols = slice(hl * dv, (hl + 1) * dv)

        ig_c = jnp.sum(jnp.where(lane == h, gates_cap, 0.0), axis=1, keepdims=True)
        b_c = jnp.sum(jnp.where(lane == heads + h, cum_cols, 0.0), axis=1, keepdims=True)
        ig_r = jnp.sum(jnp.where(sub == h, ig_rows, 0.0), axis=0, keepdims=True)
        b_r = jnp.sum(jnp.where(sub == h, cum_rows, 0.0), axis=0, keepdims=True)

        m_prev = m_ref[hl]
        d_log = jnp.where(causal, b_c - b_r + ig_r, NEG_BIG)
        g_log = b_c + m_prev
        m_t = jnp.maximum(g_log, jnp.max(d_log, axis=1, keepdims=True))
        w_intra = jnp.exp(d_log - m_t)
        w_inter = jnp.exp(g_log - m_t)

        q = q_ref[:, kcols]
        k = k_ref[:, kcols]
        v = v_ref[:, vcols]
        s = _dot_nt(q, k) * w_intra
        ct = ct_ref[hl]
        inter = _dot(q, ct.astype(BF16))
        num = _dot(s.astype(BF16), v) + w_inter * inter[:, :dv]
        den = jnp.sum(s, axis=1, keepdims=True) + w_inter * inter[:, dv:dv + 1]
        hh = num * (1.0 / jnp.maximum(jnp.abs(den), jnp.exp(-m_t)))

        b_end = b_c[L - 1:L, :]
        a_c = b_end - b_c + ig_c
        a_r = b_end - b_r + ig_r
        m_new = jnp.maximum(b_end + m_prev, jnp.max(a_r, axis=1, keepdims=True))
        decay = jnp.exp(b_end + m_prev - m_new)
        wa_c = jnp.exp(a_c - m_new)
        wv = jnp.concatenate([wa_c * v.astype(F32), wa_c * one_lane], axis=1).astype(BF16)
        ct_ref[hl] = decay * ct + _dot_tn(k, wv)
        m_ref[hl] = m_new

        hn = hh * lax.rsqrt(jnp.mean(hh * hh, axis=1, keepdims=True) + HEAD_NORM_EPS)
        out_ref[:, vcols] = (hn * ng_ref[:, vcols] * _sigmoid(o_ref[:, vcols])).astype(out_ref.dtype)


def _mlstm_core(qkv, o, gates, gates_t, norm_g, batch, seq):
    heads = MLSTM_HEADS
    assert heads == SUBLANES, "row-form gate blocks assume one sublane tile per gate kind"
    m = qkv.shape[0]
    hv = o.shape[1]
    dv = hv // heads
    dk = (qkv.shape[1] - hv) // (2 * heads)
    assert 2 * heads * dk == hv, "v blocks are addressed in units of dv"
    hps = MLSTM_HEADS_PER_STEP
    assert heads % hps == 0
    groups = heads // hps
    chunk = _tile(seq, MLSTM_CHUNK)
    nc = seq // chunk
    row = lambda b, h, c: b * nc + c
    return pl.pallas_call(
        functools.partial(_mlstm_kernel, chunk=chunk, dk=dk, dv=dv, heads=heads, hps=hps),
        grid=(batch, groups, nc),
        in_specs=[pl.BlockSpec((chunk, hps * dk), lambda b, h, c: (row(b, h, c), h)),
                  pl.BlockSpec((chunk, hps * dk), lambda b, h, c: (row(b, h, c), groups + h)),
                  pl.BlockSpec((chunk, hps * dv), lambda b, h, c: (row(b, h, c), groups + h)),
                  pl.BlockSpec((chunk, hps * dv), lambda b, h, c: (row(b, h, c), h)),
                  pl.BlockSpec((chunk, LANES), lambda b, h, c: (row(b, h, c), 0)),
                  pl.BlockSpec((SUBLANES, chunk), lambda b, h, c: (0, row(b, h, c))),
                  pl.BlockSpec((SUBLANES, chunk), lambda b, h, c: (1, row(b, h, c))),
                  pl.BlockSpec((1, hps * dv), lambda b, h, c: (0, h))],
        out_specs=pl.BlockSpec((chunk, hps * dv), lambda b, h, c: (row(b, h, c), h)),
        out_shape=jax.ShapeDtypeStruct((m, hv), BF16),
        scratch_shapes=[pltpu.VMEM((hps, dk, dv + LANES), F32), pltpu.VMEM((hps, 1, 1), F32)],
        compiler_params=_params("parallel", "parallel", "arbitrary"),
        name="mlstm_core",
    )(qkv, qkv, qkv, o, gates, gates_t, gates_t, norm_g.reshape(1, hv).astype(F32))


def _fox_cum_kernel(g_ref, cum_ref, carry_ref, *, ts):
    @pl.when(pl.program_id(1) == 0)
    def _():
        carry_ref[...] = jnp.zeros_like(carry_ref)

    logf = _log_sigmoid(g_ref[...])
    ti = lax.broadcasted_iota(jnp.int32, (ts, ts), 0)
    si = lax.broadcasted_iota(jnp.int32, (ts, ts), 1)
    tri_lo = (si <= ti).astype(BF16)
    cum = sum(_dot(tri_lo, p) for p in _split3(logf)) + carry_ref[...]
    carry_ref[...] = cum[ts - 1:ts, :]
    cum_ref[...] = cum


def _fox_cumsum(gates, batch, seq):
    ts = _tile(seq, FOX_TS)
    ns = seq // ts
    return pl.pallas_call(
        functools.partial(_fox_cum_kernel, ts=ts),
        grid=(batch, ns),
        in_specs=[pl.BlockSpec((ts, LANES), lambda b, s: (b * ns + s, 0))],
        out_specs=pl.BlockSpec((ts, LANES), lambda b, s: (b * ns + s, 0)),
        out_shape=jax.ShapeDtypeStruct((batch * seq, LANES), F32),
        scratch_shapes=[pltpu.VMEM((1, LANES), F32)],
        compiler_params=_params("parallel", "arbitrary"),
        name="fox_cumsum",
    )(gates)


def _bias_lanes(col, own_first):
    n = col.shape[0]
    hi, mid, lo = (p.astype(F32) for p in _split3(col))
    lane = lax.broadcasted_iota(jnp.int32, (n, LANES), 1)
    base = 0 if own_first else 3
    ones = ((lane >= 3 - base) & (lane < 6 - base)).astype(F32)
    out = jnp.where(lane == base, hi, jnp.where(lane == base + 1, mid, jnp.where(lane == base + 2, lo, ones)))
    return out.astype(BF16)


def _fox_attn_kernel(q_ref, k_ref, v_ref, cc_ref, o_ref, ka_scr, vt_scr, s_scr, p_scr,
                     a_scr, m_scr, l_scr, acc_scr, *, tq, seq, hps):
    hp = pl.program_id(1)
    i = pl.program_id(2)
    dh = LANES
    local_heads = range(hps)

    def head_col(rows, hl):
        lane = lax.broadcasted_iota(jnp.int32, rows.shape, 1)
        return jnp.sum(jnp.where(lane == hp * hps + hl, rows, 0.0), axis=1, keepdims=True) * LOG2E

    @pl.when(i == 0)
    def _():
        for hl in local_heads:
            hc = slice(hl * dh, (hl + 1) * dh)
            for r0 in range(0, seq, tq):
                ka_scr[hl, r0:r0 + tq, :LANES] = k_ref[r0:r0 + tq, hc]
                ka_scr[hl, r0:r0 + tq, LANES:] = _bias_lanes(-head_col(cc_ref[r0:r0 + tq, :], hl), True)
                vt_scr[hl, :, r0:r0 + tq] = v_ref[r0:r0 + tq, hc].astype(F32).T.astype(BF16)

    q0 = pl.multiple_of(i * tq, tq)
    cum_q = cc_ref[pl.ds(q0, tq), :]
    qa = [jnp.concatenate([q_ref[:, hl * dh:(hl + 1) * dh], _bias_lanes(head_col(cum_q, hl), False)], axis=1)
          for hl in local_heads]

    m_scr[...] = jnp.full_like(m_scr, NEG_BIG)
    l_scr[...] = jnp.zeros_like(l_scr)
    acc_scr[...] = jnp.zeros_like(acc_scr)
    a_scr[...] = jnp.ones_like(a_scr)
    for hl in local_heads:
        p_scr[hl, 1] = jnp.zeros((tq, tq), BF16)

    def logits(j, parity):
        start = pl.multiple_of(j * tq, tq)
        for hl in local_heads:
            s_scr[hl, parity] = _dot_nt(ka_scr[hl, pl.ds(start, tq), :], qa[hl])

    def softmax(parity, on_diagonal):
        for hl in local_heads:
            st = s_scr[hl, parity]
            if on_diagonal:
                ki = lax.broadcasted_iota(jnp.int32, (tq, tq), 0)
                qi = lax.broadcasted_iota(jnp.int32, (tq, tq), 1)
                st = jnp.where(ki <= qi, st, NEG_BIG)
            m_old = m_scr[hl]
            m_new = jnp.maximum(m_old, jnp.max(st, axis=0, keepdims=True))
            p = jnp.exp2(st - m_new)
            a = jnp.exp2(m_old - m_new)
            l_scr[hl] = a * l_scr[hl] + jnp.sum(p, axis=0, keepdims=True)
            m_scr[hl] = m_new
            p_scr[hl, parity] = p.astype(BF16)
            a_scr[hl] = a

    def values(j, parity):
        start = pl.multiple_of(jnp.maximum(j, 0) * tq, tq)
        for hl in local_heads:
            acc_scr[hl] = a_scr[hl] * acc_scr[hl] + _dot(vt_scr[hl, :, pl.ds(start, tq)], p_scr[hl, parity])

    def step(j, parity):
        values(j - 1, 1 - parity)
        logits(j + 1, 1 - parity)
        softmax(parity, False)

    def finish(parity):
        values(i - 1, 1 - parity)
        softmax(parity, True)
        values(i, parity)

    logits(0, 0)

    def pair(t, carry):
        step(2 * t, 0)
        step(2 * t + 1, 1)
        return carry

    lax.fori_loop(0, i // 2, pair, 0)

    @pl.when(i % 2 == 0)
    def _():
        finish(0)

    @pl.when(i % 2 == 1)
    def _():
        step(i - 1, 0)
        finish(1)

    for hl in local_heads:
        o_ref[:, hl * dh:(hl + 1) * dh] = (acc_scr[hl] * (1.0 / l_scr[hl])).T.astype(o_ref.dtype)


def _fox_attn(qkv, cum, batch, seq):
    dh = FOX_HEAD_DIM
    assert dh == LANES
    m = qkv.shape[0]
    heads = qkv.shape[1] // (3 * dh)
    assert heads <= LANES
    hps = FOX_HEADS_PER_STEP if heads % FOX_HEADS_PER_STEP == 0 else 1
    groups = heads // hps
    tq = _tile(seq, FOX_TQ)
    nq = seq // tq
    return pl.pallas_call(
        functools.partial(_fox_attn_kernel, tq=tq, seq=seq, hps=hps),
        grid=(batch, groups, nq),
        in_specs=[pl.BlockSpec((tq, hps * dh), lambda b, h, i: (b * nq + i, h)),
                  pl.BlockSpec((seq, hps * dh), lambda b, h, i: (b, groups + h)),
                  pl.BlockSpec((seq, hps * dh), lambda b, h, i: (b, 2 * groups + h)),
                  pl.BlockSpec((seq, LANES), lambda b, h, i: (b, 0))],
        out_specs=pl.BlockSpec((tq, hps * dh), lambda b, h, i: (b * nq + i, h)),
        out_shape=jax.ShapeDtypeStruct((m, heads * dh), BF16),
        scratch_shapes=[pltpu.VMEM((hps, seq, 2 * LANES), BF16), pltpu.VMEM((hps, dh, seq), BF16),
                        pltpu.VMEM((hps, 2, tq, tq), F32), pltpu.VMEM((hps, 2, tq, tq), BF16),
                        pltpu.VMEM((hps, 1, tq), F32), pltpu.VMEM((hps, 1, tq), F32),
                        pltpu.VMEM((hps, 1, tq), F32), pltpu.VMEM((hps, dh, tq), F32)],
        compiler_params=_params("parallel", "parallel", "arbitrary"),
        name="fox_attn",
    )(qkv, qkv, qkv, cum)


def kernel(x, mlstm_w_in, mlstm_b_gate, mlstm_norm_g, mlstm_w_out, fox_w_in, fox_b_f, fox_w_out,
           ln_mix_g, ln_mix_b, ffn_w_up, ffn_w_down, ln_ffn_g, ln_ffn_b):
    batch, seq, d = x.shape
    depth = ln_mix_g.shape[0]
    alpha = (2 * depth) ** 0.25
    m = batch * seq
    ffn = ffn_w_down.shape[1]
    ffn_pad = -(-ffn // FFN_PAD) * FFN_PAD
    hv = mlstm_w_out.shape[1]
    mlstm_main = mlstm_w_in.shape[2] - 2 * MLSTM_HEADS
    mlstm_qk = (mlstm_main - 2 * hv) // 2
    mlstm_dk = mlstm_qk // MLSTM_HEADS
    fox_w = fox_w_out.shape[1]

    mlstm_wt = jnp.swapaxes(mlstm_w_in, 1, 2)
    fox_wt = jnp.swapaxes(fox_w_in, 1, 2)

    h32 = x.reshape(m, d)
    hbf = h32.astype(BF16)
    for layer in range(depth):
        slot = layer // N_MIXERS
        if layer % N_MIXERS == 0:
            qkv, o, w_out = _proj(hbf, mlstm_wt, slot, mlstm_main - hv, hv, mlstm_qk, mlstm_dk ** -0.5,
                                  mlstm_w_out)
            gates = _gate_proj(h32, mlstm_wt, slot, mlstm_main, mlstm_b_gate[slot])
            mixed = _mlstm_core(qkv, o, gates, gates.T, mlstm_norm_g[slot], batch, seq)
        else:
            qkv, _, w_out = _proj(hbf, fox_wt, slot, 3 * fox_w, 0, fox_w, FOX_HEAD_DIM ** -0.5 * LOG2E,
                                  fox_w_out)
            gates = _gate_proj(h32, fox_wt, slot, 3 * fox_w, fox_b_f[slot])
            mixed = _fox_attn(qkv, _fox_cumsum(gates, batch, seq), batch, seq)
        h32, hbf = _mm_res_ln(mixed, w_out, h32, ln_mix_g[layer], ln_mix_b[layer], alpha)

        hidden, w_down = _ffn_up(hbf, ffn_w_up, ffn_w_down, layer, ffn_pad)
        h32, hbf = _mm_res_ln(hidden, w_down, h32, ln_ffn_g[layer], ln_ffn_b[layer], alpha)
    return h32.reshape(batch, seq, d)
```

```python
import functools
import math

import jax
import jax.numpy as jnp
from jax import lax
from jax.experimental import pallas as pl
from jax.experimental.pallas import tpu as pltpu

F32 = jnp.float32
BF16 = jnp.bfloat16

MLSTM_HEADS = 8
FOX_HEAD_DIM = 128
GATE_SOFT_CAP = 15.0
HEAD_NORM_EPS = 1e-6
LN_EPS = 1e-5
N_MIXERS = 2

LANES = 128
SUBLANES = 8
NEG_BIG = -1e30
LOG2E = math.log2(math.e)

BIG_TM = 2048
MM_TN = 512
FFN_TN = 256
LN_TM = 1024
LN_OUT_ROWS = 128
LN_TK = 1024
LN_H_CHUNKS = 8
LN_NCHUNK = 1024
LN_ROWS = 16
GATE_TM = 512
FFN_PAD = 1024
MLSTM_CHUNK = 256
MLSTM_HEADS_PER_STEP = 8
FOX_TQ = 512
FOX_TS = 512
FOX_HEADS_PER_STEP = 4
VMEM_LIMIT = 56 * 1024 * 1024


def _tile(n, pref):
    if n <= pref:
        return n
    t = (pref // LANES) * LANES
    while t >= LANES:
        if n % t == 0:
            return t
        t -= LANES
    raise ValueError(f"no lane-aligned tile of {n} below {pref}")


def _params(*sem):
    return pltpu.CompilerParams(dimension_semantics=sem, vmem_limit_bytes=VMEM_LIMIT)


def _dot(a, b):
    return jnp.dot(a, b, preferred_element_type=F32)


def _dot_nt(a, b):
    return lax.dot_general(a, b, (((1,), (1,)), ((), ())), preferred_element_type=F32)


def _dot_tn(a, b):
    return lax.dot_general(a, b, (((0,), (0,)), ((), ())), preferred_element_type=F32)


def _split3(x):
    hi = x.astype(BF16)
    r1 = x - hi.astype(F32)
    mid = r1.astype(BF16)
    lo = (r1 - mid.astype(F32)).astype(BF16)
    return hi, mid, lo


def _sigmoid(x):
    return 1.0 / (1.0 + jnp.exp(-x))


def _log_sigmoid(x):
    return jnp.minimum(x, 0.0) - jnp.log1p(jnp.exp(-jnp.abs(x)))


def _soft_cap(z):
    return GATE_SOFT_CAP * jnp.tanh(z / GATE_SOFT_CAP)


def _proj_kernel(x_ref, w_ref, side_ref, obf_ref, *more_out, n_scaled, scale, n_bf):
    j = pl.program_id(1)
    side_bf_ref = more_out[-1]

    def product():
        acc = _dot_nt(x_ref[...], w_ref[...].astype(BF16))
        side_bf_ref[...] = side_ref[...].astype(BF16)
        return acc * jnp.where(j < n_scaled, scale, 1.0) if n_scaled else acc

    if len(more_out) == 1:
        obf_ref[...] = product().astype(BF16)
        return
    o32_ref = more_out[0]

    @pl.when(j < n_bf)
    def _():
        obf_ref[...] = product().astype(BF16)

    @pl.when(j >= n_bf)
    def _():
        o32_ref[...] = product()


def _proj(x, wt_stack, idx, bf_cols, f32_cols, scaled_cols, scale, side_stack):
    m, k = x.shape
    tm, tn = _tile(m, BIG_TM), _tile(math.gcd(bf_cols, f32_cols), MM_TN)
    assert bf_cols % tn == 0 and f32_cols % tn == 0 and scaled_cols % tn == 0
    n_bf = bf_cols // tn
    ni, nj = m // tm, (bf_cols + f32_cols) // tn
    _, rows, cols = side_stack.shape
    sr = next(r for r in range(2 * SUBLANES, rows + 1, 2 * SUBLANES)
              if rows % r == 0 and r * ni * nj >= rows)
    nblocks = rows // sr
    block = lambda i, j: jnp.minimum(i * nj + j, nblocks - 1)
    out_specs = [pl.BlockSpec((tm, tn), lambda i, j: (i, jnp.minimum(j, n_bf - 1)))]
    out_shape = [jax.ShapeDtypeStruct((m, bf_cols), BF16)]
    if f32_cols:
        out_specs.append(pl.BlockSpec((tm, tn), lambda i, j: (i, jnp.maximum(j - n_bf, 0))))
        out_shape.append(jax.ShapeDtypeStruct((m, f32_cols), F32))
    out_specs.append(pl.BlockSpec((sr, cols), lambda i, j: (block(i, j), 0)))
    out_shape.append(jax.ShapeDtypeStruct((rows, cols), BF16))
    res = pl.pallas_call(
        functools.partial(_proj_kernel, n_scaled=scaled_cols // tn, scale=scale, n_bf=n_bf),
        grid=(ni, nj),
        in_specs=[pl.BlockSpec((tm, k), lambda i, j: (i, 0), pipeline_mode=pl.Buffered(1)),
                  pl.BlockSpec((None, tn, k), lambda i, j: (idx, j, 0)),
                  pl.BlockSpec((None, sr, cols), lambda i, j: (idx, block(i, j), 0))],
        out_specs=out_specs,
        out_shape=out_shape,
        compiler_params=_params("arbitrary", "arbitrary"),
        name="proj",
    )(x, wt_stack, side_stack)
    return (res[0], res[1], res[2]) if f32_cols else (res[0], None, res[1])


def _gate_kernel(h_ref, w_ref, b_ref, o_ref, *maybe_xbf_ref, n_gates):
    x = h_ref[...]
    w = w_ref[...]
    x_hi = x.astype(BF16)
    for xbf_ref in maybe_xbf_ref:
        xbf_ref[...] = x_hi
    x_lo = (x - x_hi.astype(F32)).astype(BF16)
    w_hi = w.astype(BF16)
    w_lo = (w - w_hi.astype(F32)).astype(BF16)
    both = _dot_nt(x_hi, jnp.concatenate([w_hi, w_lo], axis=0))
    acc = both[:, :LANES] + both[:, LANES:] + _dot_nt(x_lo, w_hi)
    lane = lax.broadcasted_iota(jnp.int32, acc.shape, 1)
    o_ref[...] = jnp.where(lane < n_gates, acc + b_ref[...], 0.0)


def _gate_proj(h32, wt_stack, idx, col0, bias, emit_bf16=False):
    m, d = h32.shape
    g = wt_stack.shape[1] - col0
    assert col0 % LANES == 0 and 0 < g <= LANES
    b_pad = jnp.pad(bias.astype(F32), (0, LANES - g)).reshape(1, LANES)
    tm = _tile(m, GATE_TM)
    out_specs = [pl.BlockSpec((tm, LANES), lambda i: (i, 0))]
    out_shape = [jax.ShapeDtypeStruct((m, LANES), F32)]
    if emit_bf16:
        out_specs.append(pl.BlockSpec((tm, d), lambda i: (i, 0)))
        out_shape.append(jax.ShapeDtypeStruct((m, d), BF16))
    res = pl.pallas_call(
        functools.partial(_gate_kernel, n_gates=g),
        grid=(m // tm,),
        in_specs=[pl.BlockSpec((tm, d), lambda i: (i, 0)),
                  pl.BlockSpec((None, LANES, d), lambda i: (idx, col0 // LANES, 0)),
                  pl.BlockSpec((1, LANES), lambda i: (0, 0))],
        out_specs=out_specs,
        out_shape=out_shape,
        compiler_params=_params("parallel"),
        name="gate_proj",
    )(h32, wt_stack, b_pad)
    return res if emit_bf16 else res[0]


def _mm_res_ln_kernel(a_ref, b_ref, h_ref, g_ref, beta_ref, o32_ref, obf_ref,
                      acc_ref, part_scr, mu_scr, rstd_scr, *, alpha, nk, nh):
    t = pl.program_id(1)
    tm, d = acc_ref.shape
    rows_out = o32_ref.shape[0]
    nch = min(d, LN_NCHUNK)
    chunks = range(0, d, nch)

    @pl.when(t == 0)
    def _():
        a = a_ref[...]
        for n0 in chunks:
            acc_ref[:, n0:n0 + nch] = _dot(a, b_ref[:, n0:n0 + nch])

    @pl.when((t > 0) & (t < nk))
    def _():
        a = a_ref[...]
        for n0 in chunks:
            acc_ref[:, n0:n0 + nch] += _dot(a, b_ref[:, n0:n0 + nch])

    @pl.when(t < nh)
    def _():
        hc = d // nh
        cols = pl.ds(pl.multiple_of(t * hc, hc), hc)
        acc_ref[:, cols] += alpha * h_ref[...]

    @pl.when(t >= nk)
    def _():
        rb = min(rows_out, LN_ROWS)
        row0 = (t - nk) * rows_out
        cols = [slice(c, c + LANES) for c in range(0, d, LANES)]

        def row_groups(fn):
            def body(r, carry):
                local = pl.multiple_of(r * rb, rb)
                fn(pl.ds(local, rb), pl.ds(pl.multiple_of(row0 + local, rb), rb))
                return carry
            lax.fori_loop(0, rows_out // rb, body, 0)

        def lane_mean(part):
            total = jnp.sum(part, axis=1, keepdims=True) * (1.0 / d)
            return jnp.broadcast_to(total, part.shape)

        def tree_sum(xs):
            while len(xs) > 1:
                xs = [xs[n] + xs[n + 1] for n in range(0, len(xs) - 1, 2)] + xs[len(xs) & ~1:]
            return xs[0]

        def pass_sum(rows, src):
            part_scr[rows, :] = tree_sum([acc_ref[src, c] for c in cols])

        def pass_sq(rows, src):
            mu = mu_scr[rows, :]
            dev = [acc_ref[src, c] - mu for c in cols]
            part_scr[rows, :] = tree_sum([x * x for x in dev])

        def pass_norm(rows, src):
            mu = mu_scr[rows, :]
            rstd = rstd_scr[rows, :]
            for c in cols:
                out = (acc_ref[src, c] - mu) * rstd * g_ref[:, c] + beta_ref[:, c]
                o32_ref[rows, c] = out
                obf_ref[rows, c] = out.astype(BF16)

        row_groups(pass_sum)
        mu_scr[...] = lane_mean(part_scr[...])
        row_groups(pass_sq)
        rstd_scr[...] = lax.rsqrt(lane_mean(part_scr[...]) + LN_EPS)
        row_groups(pass_norm)


def _mm_res_ln(a, b, h32, g, beta, alpha):
    m, kdim = a.shape
    d = b.shape[1]
    tm, tk = _tile(m, LN_TM), _tile(kdim, LN_TK)
    nk = kdim // tk
    nh = max(n for n in range(1, min(nk, LN_H_CHUNKS) + 1) if d % (n * LANES) == 0)
    rows_out = min(tm, LN_OUT_ROWS)
    ns = tm // rows_out
    ni = m // tm
    sweep = lambda t: t >= nk
    i_of = lambda i, t: jnp.where(sweep(t), jnp.minimum(i + 1, ni - 1), i)
    k_of = lambda t: jnp.where(sweep(t), 0, t)
    out_row = lambda i, t: (i * ns + jnp.maximum(t - nk, 0), 0)
    return pl.pallas_call(
        functools.partial(_mm_res_ln_kernel, alpha=alpha, nk=nk, nh=nh),
        grid=(ni, nk + ns),
        in_specs=[pl.BlockSpec((tm, tk), lambda i, t: (i_of(i, t), k_of(t))),
                  pl.BlockSpec((tk, d), lambda i, t: (k_of(t), 0)),
                  pl.BlockSpec((tm, d // nh), lambda i, t: (i_of(i, t), jnp.minimum(k_of(t), nh - 1))),
                  pl.BlockSpec((1, d), lambda i, t: (0, 0)),
                  pl.BlockSpec((1, d), lambda i, t: (0, 0))],
        out_specs=[pl.BlockSpec((rows_out, d), out_row),
                   pl.BlockSpec((rows_out, d), out_row)],
        out_shape=[jax.ShapeDtypeStruct((m, d), F32), jax.ShapeDtypeStruct((m, d), BF16)],
        scratch_shapes=[pltpu.VMEM((tm, d), F32)] + [pltpu.VMEM((rows_out, LANES), F32)] * 3,
        compiler_params=_params("parallel", "arbitrary"),
        name="proj_res_ln",
    )(a, b, h32, g.reshape(1, d).astype(F32), beta.reshape(1, d).astype(F32))


def _ffn_up_kernel(x_ref, wg_ref, wu_ref, wd_ref, o_ref, wd_bf_ref, *, n_real, nj, wd_blocks):
    i = pl.program_id(0)
    j = pl.program_id(1)

    def cast_down_rows():
        real_rows = i * nj + j < wd_blocks
        wd_bf_ref[...] = jnp.where(real_rows, wd_ref[...], 0.0).astype(BF16)

    @pl.when(j < n_real)
    def _():
        x = x_ref[...]
        gate = _dot(x, wg_ref[...].astype(BF16))
        up = _dot(x, wu_ref[...].astype(BF16))
        o_ref[...] = (gate * _sigmoid(gate) * up).astype(o_ref.dtype)
        cast_down_rows()

    @pl.when(j >= n_real)
    def _():
        o_ref[...] = jnp.zeros_like(o_ref)
        cast_down_rows()


def _ffn_up(x, w_up_stack, w_down_stack, idx, ffn_pad):
    m, d = x.shape
    f = w_up_stack.shape[2] // 2
    d_out = w_down_stack.shape[2]
    tm, tn = _tile(m, BIG_TM), _tile(f, FFN_TN)
    assert ffn_pad % tn == 0
    n_real = f // tn
    ni, nj = m // tm, ffn_pad // tn
    wd_rows = ffn_pad // (ni * nj)
    assert wd_rows * ni * nj == ffn_pad and wd_rows % (2 * SUBLANES) == 0 and f % wd_rows == 0
    wd_blocks = f // wd_rows
    real = lambda j: jnp.minimum(j, n_real - 1)
    return pl.pallas_call(
        functools.partial(_ffn_up_kernel, n_real=n_real, nj=nj, wd_blocks=wd_blocks),
        grid=(ni, nj),
        in_specs=[pl.BlockSpec((tm, d), lambda i, j: (i, 0), pipeline_mode=pl.Buffered(1)),
                  pl.BlockSpec((None, d, tn), lambda i, j: (idx, 0, real(j))),
                  pl.BlockSpec((None, d, tn), lambda i, j: (idx, 0, n_real + real(j))),
                  pl.BlockSpec((None, wd_rows, d_out),
                               lambda i, j: (idx, jnp.minimum(i * nj + j, wd_blocks - 1), 0))],
        out_specs=[pl.BlockSpec((tm, tn), lambda i, j: (i, j)),
                   pl.BlockSpec((wd_rows, d_out), lambda i, j: (i * nj + j, 0))],
        out_shape=[jax.ShapeDtypeStruct((m, ffn_pad), BF16),
                   jax.ShapeDtypeStruct((ffn_pad, d_out), BF16)],
        compiler_params=_params("arbitrary", "arbitrary"),
        name="ffn_up",
    )(x, w_up_stack, w_up_stack, w_down_stack)


def _mlstm_kernel(q_ref, k_ref, v_ref, o_ref, gc_ref, gri_ref, grf_ref, ng_ref, out_ref,
                  ct_ref, m_ref, *, chunk, dk, dv, heads, hps):
    hp = pl.program_id(1)
    c = pl.program_id(2)
    L = chunk

    @pl.when(c == 0)
    def _():
        ct_ref[...] = jnp.zeros_like(ct_ref)
        m_ref[...] = jnp.zeros_like(m_ref)

    lane = lax.broadcasted_iota(jnp.int32, (L, LANES), 1)
    sub = lax.broadcasted_iota(jnp.int32, (SUBLANES, L), 0)
    ti = lax.broadcasted_iota(jnp.int32, (L, L), 0)
    si = lax.broadcasted_iota(jnp.int32, (L, L), 1)
    causal = si <= ti
    tri_lo = causal.astype(BF16)
    tri_up = (ti <= si).astype(BF16)
    one_lane = (lane == 0).astype(F32)

    gates_cap = _soft_cap(gc_ref[...])
    cum_cols = sum(_dot(tri_lo, p) for p in _split3(_log_sigmoid(gates_cap)))
    ig_rows = _soft_cap(gri_ref[...])
    cum_rows = sum(_dot(p, tri_up) for p in _split3(_log_sigmoid(_soft_cap(grf_ref[...]))))

    for hl in range(hps):
        h = hp * hps + hl
        kcols = slice(hl * dk, (hl + 1) * dk)
        vcols = slice(hl * dv, (hl + 1) * dv)

        ig_c = jnp.sum(jnp.where(lane == h, gates_cap, 0.0), axis=1, keepdims=True)
        b_c = jnp.sum(jnp.where(lane == heads + h, cum_cols, 0.0), axis=1, keepdims=True)
        ig_r = jnp.sum(jnp.where(sub == h, ig_rows, 0.0), axis=0, keepdims=True)
        b_r = jnp.sum(jnp.where(sub == h, cum_rows, 0.0), axis=0, keepdims=True)

        m_prev = m_ref[hl]
        d_log = jnp.where(causal, b_c - b_r + ig_r, NEG_BIG)
        g_log = b_c + m_prev
        m_t = jnp.maximum(g_log, jnp.max(d_log, axis=1, keepdims=True))
        w_intra = jnp.exp(d_log - m_t)
        w_inter = jnp.exp(g_log - m_t)

        q = q_ref[:, kcols]
        k = k_ref[:, kcols]
        v = v_ref[:, vcols]
        s = _dot_nt(q, k) * w_intra
        ct = ct_ref[hl]
        inter = _dot(q, ct.astype(BF16))
        num = _dot(s.astype(BF16), v) + w_inter * inter[:, :dv]
        den = jnp.sum(s, axis=1, keepdims=True) + w_inter * inter[:, dv:dv + 1]
        hh = num * (1.0 / jnp.maximum(jnp.abs(den), jnp.exp(-m_t)))

        b_end = b_c[L - 1:L, :]
        a_c = b_end - b_c + ig_c
        a_r = b_end - b_r + ig_r
        m_new = jnp.maximum(b_end + m_prev, jnp.max(a_r, axis=1, keepdims=True))
        decay = jnp.exp(b_end + m_prev - m_new)
        wa_c = jnp.exp(a_c - m_new)
        wv = jnp.concatenate([wa_c * v.astype(F32), wa_c * one_lane], axis=1).astype(BF16)
        ct_ref[hl] = decay * ct + _dot_tn(k, wv)
        m_ref[hl] = m_new

        hn = hh * lax.rsqrt(jnp.mean(hh * hh, axis=1, keepdims=True) + HEAD_NORM_EPS)
        out_ref[:, vcols] = (hn * ng_ref[:, vcols] * _sigmoid(o_ref[:, vcols])).astype(out_ref.dtype)


def _mlstm_core(qkv, o, gates, gates_t, norm_g, batch, seq):
    heads = MLSTM_HEADS
    assert heads == SUBLANES, "row-form gate blocks assume one sublane tile per gate kind"
    m = qkv.shape[0]
    hv = o.shape[1]
    dv = hv // heads
    dk = (qkv.shape[1] - hv) // (2 * heads)
    assert 2 * heads * dk == hv, "v blocks are addressed in units of dv"
    hps = MLSTM_HEADS_PER_STEP
    assert heads % hps == 0
    groups = heads // hps
    chunk = _tile(seq, MLSTM_CHUNK)
    nc = seq // chunk
    row = lambda b, h, c: b * nc + c
    return pl.pallas_call(
        functools.partial(_mlstm_kernel, chunk=chunk, dk=dk, dv=dv, heads=heads, hps=hps),
        grid=(batch, groups, nc),
        in_specs=[pl.BlockSpec((chunk, hps * dk), lambda b, h, c: (row(b, h, c), h)),
                  pl.BlockSpec((chunk, hps * dk), lambda b, h, c: (row(b, h, c), groups + h)),
                  pl.BlockSpec((chunk, hps * dv), lambda b, h, c: (row(b, h, c), groups + h)),
                  pl.BlockSpec((chunk, hps * dv), lambda b, h, c: (row(b, h, c), h)),
                  pl.BlockSpec((chunk, LANES), lambda b, h, c: (row(b, h, c), 0)),
                  pl.BlockSpec((SUBLANES, chunk), lambda b, h, c: (0, row(b, h, c))),
                  pl.BlockSpec((SUBLANES, chunk), lambda b, h, c: (1, row(b, h, c))),
                  pl.BlockSpec((1, hps * dv), lambda b, h, c: (0, h))],
        out_specs=pl.BlockSpec((chunk, hps * dv), lambda b, h, c: (row(b, h, c), h)),
        out_shape=jax.ShapeDtypeStruct((m, hv), BF16),
        scratch_shapes=[pltpu.VMEM((hps, dk, dv + LANES), F32), pltpu.VMEM((hps, 1, 1), F32)],
        compiler_params=_params("parallel", "parallel", "arbitrary"),
        name="mlstm_core",
    )(qkv, qkv, qkv, o, gates, gates_t, gates_t, norm_g.reshape(1, hv).astype(F32))


def _fox_cum_kernel(g_ref, cum_ref, carry_ref, *, ts):
    @pl.when(pl.program_id(1) == 0)
    def _():
        carry_ref[...] = jnp.zeros_like(carry_ref)

    logf = _log_sigmoid(g_ref[...])
    ti = lax.broadcasted_iota(jnp.int32, (ts, ts), 0)
    si = lax.broadcasted_iota(jnp.int32, (ts, ts), 1)
    tri_lo = (si <= ti).astype(BF16)
    cum = sum(_dot(tri_lo, p) for p in _split3(logf)) + carry_ref[...]
    carry_ref[...] = cum[ts - 1:ts, :]
    cum_ref[...] = cum


def _fox_cumsum(gates, batch, seq):
    ts = _tile(seq, FOX_TS)
    ns = seq // ts
    return pl.pallas_call(
        functools.partial(_fox_cum_kernel, ts=ts),
        grid=(batch, ns),
        in_specs=[pl.BlockSpec((ts, LANES), lambda b, s: (b * ns + s, 0))],
        out_specs=pl.BlockSpec((ts, LANES), lambda b, s: (b * ns + s, 0)),
        out_shape=jax.ShapeDtypeStruct((batch * seq, LANES), F32),
        scratch_shapes=[pltpu.VMEM((1, LANES), F32)],
        compiler_params=_params("parallel", "arbitrary"),
        name="fox_cumsum",
    )(gates)


def _bias_lanes(col, own_first):
    n = col.shape[0]
    hi, mid, lo = (p.astype(F32) for p in _split3(col))
    lane = lax.broadcasted_iota(jnp.int32, (n, LANES), 1)
    base = 0 if own_first else 3
    ones = ((lane >= 3 - base) & (lane < 6 - base)).astype(F32)
    out = jnp.where(lane == base, hi, jnp.where(lane == base + 1, mid, jnp.where(lane == base + 2, lo, ones)))
    return out.astype(BF16)


def _fox_attn_kernel(q_ref, k_ref, v_ref, cc_ref, o_ref, ka_scr, vt_scr, s_scr, p_scr,
                     a_scr, m_scr, l_scr, acc_scr, *, tq, seq, hps):
    hp = pl.program_id(1)
    i = pl.program_id(2)
    dh = LANES
    local_heads = range(hps)

    def head_col(rows, hl):
        lane = lax.broadcasted_iota(jnp.int32, rows.shape, 1)
        return jnp.sum(jnp.where(lane == hp * hps + hl, rows, 0.0), axis=1, keepdims=True) * LOG2E

    @pl.when(i == 0)
    def _():
        for hl in local_heads:
            hc = slice(hl * dh, (hl + 1) * dh)
            for r0 in range(0, seq, tq):
                ka_scr[hl, r0:r0 + tq, :LANES] = k_ref[r0:r0 + tq, hc]
                ka_scr[hl, r0:r0 + tq, LANES:] = _bias_lanes(-head_col(cc_ref[r0:r0 + tq, :], hl), True)
                vt_scr[hl, :, r0:r0 + tq] = v_ref[r0:r0 + tq, hc].astype(F32).T.astype(BF16)

    q0 = pl.multiple_of(i * tq, tq)
    cum_q = cc_ref[pl.ds(q0, tq), :]
    qa = [jnp.concatenate([q_ref[:, hl * dh:(hl + 1) * dh], _bias_lanes(head_col(cum_q, hl), False)], axis=1)
          for hl in local_heads]

    m_scr[...] = jnp.full_like(m_scr, NEG_BIG)
    l_scr[...] = jnp.zeros_like(l_scr)
    acc_scr[...] = jnp.zeros_like(acc_scr)
    a_scr[...] = jnp.ones_like(a_scr)
    for hl in local_heads:
        p_scr[hl, 1] = jnp.zeros((tq, tq), BF16)

    def logits(j, parity):
        start = pl.multiple_of(j * tq, tq)
        for hl in local_heads:
            s_scr[hl, parity] = _dot_nt(ka_scr[hl, pl.ds(start, tq), :], qa[hl])

    def softmax(parity, on_diagonal):
        for hl in local_heads:
            st = s_scr[hl, parity]
            if on_diagonal:
                ki = lax.broadcasted_iota(jnp.int32, (tq, tq), 0)
                qi = lax.broadcasted_iota(jnp.int32, (tq, tq), 1)
                st = jnp.where(ki <= qi, st, NEG_BIG)
            m_old = m_scr[hl]
            m_new = jnp.maximum(m_old, jnp.max(st, axis=0, keepdims=True))
            p = jnp.exp2(st - m_new)
            a = jnp.exp2(m_old - m_new)
            l_scr[hl] = a * l_scr[hl] + jnp.sum(p, axis=0, keepdims=True)
            m_scr[hl] = m_new
            p_scr[hl, parity] = p.astype(BF16)
            a_scr[hl] = a

    def values(j, parity):
        start = pl.multiple_of(jnp.maximum(j, 0) * tq, tq)
        for hl in local_heads:
            acc_scr[hl] = a_scr[hl] * acc_scr[hl] + _dot(vt_scr[hl, :, pl.ds(start, tq)], p_scr[hl, parity])

    def step(j, parity):
        values(j - 1, 1 - parity)
        logits(j + 1, 1 - parity)
        softmax(parity, False)

    def finish(parity):
        values(i - 1, 1 - parity)
        softmax(parity, True)
        values(i, parity)

    logits(0, 0)

    def pair(t, carry):
        step(2 * t, 0)
        step(2 * t + 1, 1)
        return carry

    lax.fori_loop(0, i // 2, pair, 0)

    @pl.when(i % 2 == 0)
    def _():
        finish(0)

    @pl.when(i % 2 == 1)
    def _():
        step(i - 1, 0)
        finish(1)

    for hl in local_heads:
        o_ref[:, hl * dh:(hl + 1) * dh] = (acc_scr[hl] * (1.0 / l_scr[hl])).T.astype(o_ref.dtype)


def _fox_attn(qkv, cum, batch, seq):
    dh = FOX_HEAD_DIM
    assert dh == LANES
    m = qkv.shape[0]
    heads = qkv.shape[1] // (3 * dh)
    assert heads <= LANES
    hps = FOX_HEADS_PER_STEP if heads % FOX_HEADS_PER_STEP == 0 else 1
    groups = heads // hps
    tq = _tile(seq, FOX_TQ)
    nq = seq // tq
    return pl.pallas_call(
        functools.partial(_fox_attn_kernel, tq=tq, seq=seq, hps=hps),
        grid=(batch, groups, nq),
        in_specs=[pl.BlockSpec((tq, hps * dh), lambda b, h, i: (b * nq + i, h)),
                  pl.BlockSpec((seq, hps * dh), lambda b, h, i: (b, groups + h)),
                  pl.BlockSpec((seq, hps * dh), lambda b, h, i: (b, 2 * groups + h)),
                  pl.BlockSpec((seq, LANES), lambda b, h, i: (b, 0))],
        out_specs=pl.BlockSpec((tq, hps * dh), lambda b, h, i: (b * nq + i, h)),
        out_shape=jax.ShapeDtypeStruct((m, heads * dh), BF16),
        scratch_shapes=[pltpu.VMEM((hps, seq, 2 * LANES), BF16), pltpu.VMEM((hps, dh, seq), BF16),
                        pltpu.VMEM((hps, 2, tq, tq), F32), pltpu.VMEM((hps, 2, tq, tq), BF16),
                        pltpu.VMEM((hps, 1, tq), F32), pltpu.VMEM((hps, 1, tq), F32),
                        pltpu.VMEM((hps, 1, tq), F32), pltpu.VMEM((hps, dh, tq), F32)],
        compiler_params=_params("parallel", "parallel", "arbitrary"),
        name="fox_attn",
    )(qkv, qkv, qkv, cum)


def kernel(x, mlstm_w_in, mlstm_b_gate, mlstm_norm_g, mlstm_w_out, fox_w_in, fox_b_f, fox_w_out,
           ln_mix_g, ln_mix_b, ffn_w_up, ffn_w_down, ln_ffn_g, ln_ffn_b):
    batch, seq, d = x.shape
    depth = ln_mix_g.shape[0]
    alpha = (2 * depth) ** 0.25
    m = batch * seq
    ffn = ffn_w_down.shape[1]
    ffn_pad = -(-ffn // FFN_PAD) * FFN_PAD
    hv = mlstm_w_out.shape[1]
    mlstm_main = mlstm_w_in.shape[2] - 2 * MLSTM_HEADS
    mlstm_qk = (mlstm_main - 2 * hv) // 2
    mlstm_dk = mlstm_qk // MLSTM_HEADS
    fox_w = fox_w_out.shape[1]

    mlstm_wt = jnp.swapaxes(mlstm_w_in, 1, 2)
    fox_wt = jnp.swapaxes(fox_w_in, 1, 2)

    h32 = x.reshape(m, d)
    hbf = None
    for layer in range(depth):
        slot = layer // N_MIXERS
        if layer % N_MIXERS == 0:
            gates = _gate_proj(h32, mlstm_wt, slot, mlstm_main, mlstm_b_gate[slot], emit_bf16=hbf is None)
            if hbf is None:
                gates, hbf = gates
            qkv, o, w_out = _proj(hbf, mlstm_wt, slot, mlstm_main - hv, hv, mlstm_qk, mlstm_dk ** -0.5,
                                  mlstm_w_out)
            mixed = _mlstm_core(qkv, o, gates, gates.T, mlstm_norm_g[slot], batch, seq)
        else:
            qkv, _, w_out = _proj(hbf, fox_wt, slot, 3 * fox_w, 0, fox_w, FOX_HEAD_DIM ** -0.5 * LOG2E,
                                  fox_w_out)
            gates = _gate_proj(h32, fox_wt, slot, 3 * fox_w, fox_b_f[slot])
            mixed = _fox_attn(qkv, _fox_cumsum(gates, batch, seq), batch, seq)
        h32, hbf = _mm_res_ln(mixed, w_out, h32, ln_mix_g[layer], ln_mix_b[layer], alpha)

        hidden, w_down = _ffn_up(hbf, ffn_w_up, ffn_w_down, layer, ffn_pad)
        h32, hbf = _mm_res_ln(hidden, w_down, h32, ln_ffn_g[layer], ln_ffn_b[layer], alpha)
    return h32.reshape(batch, seq, d)
```

```python
import functools
import math

import jax
import jax.numpy as jnp
from jax import lax
from jax.experimental import pallas as pl
from jax.experimental.pallas import tpu as pltpu

F32 = jnp.float32
BF16 = jnp.bfloat16

MLSTM_HEADS = 8
FOX_HEAD_DIM = 128
GATE_SOFT_CAP = 15.0
HEAD_NORM_EPS = 1e-6
LN_EPS = 1e-5
N_MIXERS = 2

LANES = 128
SUBLANES = 8
NEG_BIG = -1e30
LOG2E = math.log2(math.e)

BIG_TM = 2048
MM_TN = 512
FFN_TN = 256
LN_TM = 1024
LN_OUT_ROWS = 128
LN_TK = 1024
LN_H_CHUNKS = 8
LN_NCHUNK = 1024
LN_ROWS = 16
GATE_TM = 512
FFN_PAD = 1024
MLSTM_CHUNK = 256
MLSTM_HEADS_PER_STEP = 8
FOX_TQ = 512
FOX_TS = 512
FOX_HEADS_PER_STEP = 4
VMEM_LIMIT = 56 * 1024 * 1024


def _tile(n, pref):
    if n <= pref:
        return n
    t = (pref // LANES) * LANES
    while t >= LANES:
        if n % t == 0:
            return t
        t -= LANES
    raise ValueError(f"no lane-aligned tile of {n} below {pref}")


def _params(*sem):
    return pltpu.CompilerParams(dimension_semantics=sem, vmem_limit_bytes=VMEM_LIMIT)


def _dot(a, b):
    return jnp.dot(a, b, preferred_element_type=F32)


def _dot_nt(a, b):
    return lax.dot_general(a, b, (((1,), (1,)), ((), ())), preferred_element_type=F32)


def _dot_tn(a, b):
    return lax.dot_general(a, b, (((0,), (0,)), ((), ())), preferred_element_type=F32)


def _split3(x):
    hi = x.astype(BF16)
    r1 = x - hi.astype(F32)
    mid = r1.astype(BF16)
    lo = (r1 - mid.astype(F32)).astype(BF16)
    return hi, mid, lo


def _sigmoid(x):
    return 1.0 / (1.0 + jnp.exp(-x))


def _log_sigmoid(x):
    return jnp.minimum(x, 0.0) - jnp.log1p(jnp.exp(-jnp.abs(x)))


def _soft_cap(z):
    return GATE_SOFT_CAP * jnp.tanh(z / GATE_SOFT_CAP)


def _proj_kernel(x_ref, w_ref, side_ref, obf_ref, *more_out, n_scaled, scale, n_bf):
    j = pl.program_id(1)
    side_bf_ref = more_out[-1]

    def product():
        acc = _dot_nt(x_ref[...], w_ref[...].astype(BF16))
        side_bf_ref[...] = side_ref[...].astype(BF16)
        return acc * jnp.where(j < n_scaled, scale, 1.0) if n_scaled else acc

    if len(more_out) == 1:
        obf_ref[...] = product().astype(BF16)
        return
    o32_ref = more_out[0]

    @pl.when(j < n_bf)
    def _():
        obf_ref[...] = product().astype(BF16)

    @pl.when(j >= n_bf)
    def _():
        o32_ref[...] = product()


def _proj(x, wt_stack, idx, bf_cols, f32_cols, scaled_cols, scale, side_stack):
    m, k = x.shape
    tm, tn = _tile(m, BIG_TM), _tile(math.gcd(bf_cols, f32_cols), MM_TN)
    assert bf_cols % tn == 0 and f32_cols % tn == 0 and scaled_cols % tn == 0
    n_bf = bf_cols // tn
    ni, nj = m // tm, (bf_cols + f32_cols) // tn
    _, rows, cols = side_stack.shape
    sr = next(r for r in range(2 * SUBLANES, rows + 1, 2 * SUBLANES)
              if rows % r == 0 and r * ni * nj >= rows)
    nblocks = rows // sr
    block = lambda i, j: jnp.minimum(i * nj + j, nblocks - 1)
    out_specs = [pl.BlockSpec((tm, tn), lambda i, j: (i, jnp.minimum(j, n_bf - 1)))]
    out_shape = [jax.ShapeDtypeStruct((m, bf_cols), BF16)]
    if f32_cols:
        out_specs.append(pl.BlockSpec((tm, tn), lambda i, j: (i, jnp.maximum(j - n_bf, 0))))
        out_shape.append(jax.ShapeDtypeStruct((m, f32_cols), F32))
    out_specs.append(pl.BlockSpec((sr, cols), lambda i, j: (block(i, j), 0)))
    out_shape.append(jax.ShapeDtypeStruct((rows, cols), BF16))
    res = pl.pallas_call(
        functools.partial(_proj_kernel, n_scaled=scaled_cols // tn, scale=scale, n_bf=n_bf),
        grid=(ni, nj),
        in_specs=[pl.BlockSpec((tm, k), lambda i, j: (i, 0), pipeline_mode=pl.Buffered(1)),
                  pl.BlockSpec((None, tn, k), lambda i, j: (idx, j, 0)),
                  pl.BlockSpec((None, sr, cols), lambda i, j: (idx, block(i, j), 0))],
        out_specs=out_specs,
        out_shape=out_shape,
        compiler_params=_params("arbitrary", "arbitrary"),
        name="proj",
    )(x, wt_stack, side_stack)
    return (res[0], res[1], res[2]) if f32_cols else (res[0], None, res[1])


def _gate_kernel(h_ref, w_ref, b_ref, o_ref, *maybe_xbf_ref, n_gates):
    x = h_ref[...]
    w = w_ref[...]
    x_hi = x.astype(BF16)
    for xbf_ref in maybe_xbf_ref:
        xbf_ref[...] = x_hi
    x_lo = (x - x_hi.astype(F32)).astype(BF16)
    w_hi = w.astype(BF16)
    w_lo = (w - w_hi.astype(F32)).astype(BF16)
    both = _dot_nt(x_hi, jnp.concatenate([w_hi, w_lo], axis=0))
    acc = both[:, :LANES] + both[:, LANES:] + _dot_nt(x_lo, w_hi)
    lane = lax.broadcasted_iota(jnp.int32, acc.shape, 1)
    o_ref[...] = jnp.where(lane < n_gates, acc + b_ref[...], 0.0)


def _gate_proj(h32, wt_stack, idx, col0, bias, emit_bf16=False):
    m, d = h32.shape
    g = wt_stack.shape[1] - col0
    assert col0 % LANES == 0 and 0 < g <= LANES
    b_pad = jnp.pad(bias.astype(F32), (0, LANES - g)).reshape(1, LANES)
    tm = _tile(m, GATE_TM)
    out_specs = [pl.BlockSpec((tm, LANES), lambda i: (i, 0))]
    out_shape = [jax.ShapeDtypeStruct((m, LANES), F32)]
    if emit_bf16:
        out_specs.append(pl.BlockSpec((tm, d), lambda i: (i, 0)))
        out_shape.append(jax.ShapeDtypeStruct((m, d), BF16))
    res = pl.pallas_call(
        functools.partial(_gate_kernel, n_gates=g),
        grid=(m // tm,),
        in_specs=[pl.BlockSpec((tm, d), lambda i: (i, 0)),
                  pl.BlockSpec((None, LANES, d), lambda i: (idx, col0 // LANES, 0)),
                  pl.BlockSpec((1, LANES), lambda i: (0, 0))],
        out_specs=out_specs,
        out_shape=out_shape,
        compiler_params=_params("parallel"),
        name="gate_proj",
    )(h32, wt_stack, b_pad)
    return res if emit_bf16 else res[0]


def _mm_res_ln_kernel(a_ref, b_ref, h_ref, g_ref, beta_ref, o32_ref, obf_ref,
                      acc_ref, part_scr, mu_scr, rstd_scr, *, alpha, nk, nh):
    t = pl.program_id(1)
    tm, d = acc_ref.shape
    rows_out = o32_ref.shape[0]
    nch = min(d, LN_NCHUNK)
    chunks = range(0, d, nch)

    @pl.when(t == 0)
    def _():
        a = a_ref[...]
        for n0 in chunks:
            acc_ref[:, n0:n0 + nch] = _dot(a, b_ref[:, n0:n0 + nch])

    @pl.when((t > 0) & (t < nk))
    def _():
        a = a_ref[...]
        for n0 in chunks:
            acc_ref[:, n0:n0 + nch] += _dot(a, b_ref[:, n0:n0 + nch])

    @pl.when(t < nh)
    def _():
        hc = d // nh
        cols = pl.ds(pl.multiple_of(t * hc, hc), hc)
        acc_ref[:, cols] += alpha * h_ref[...]

    @pl.when(t >= nk)
    def _():
        rb = min(rows_out, LN_ROWS)
        row0 = (t - nk) * rows_out
        cols = [slice(c, c + LANES) for c in range(0, d, LANES)]

        def row_groups(fn):
            def body(r, carry):
                local = pl.multiple_of(r * rb, rb)
                fn(pl.ds(local, rb), pl.ds(pl.multiple_of(row0 + local, rb), rb))
                return carry
            lax.fori_loop(0, rows_out // rb, body, 0)

        def lane_mean(part):
            total = jnp.sum(part, axis=1, keepdims=True) * (1.0 / d)
            return jnp.broadcast_to(total, part.shape)

        def tree_sum(xs):
            while len(xs) > 1:
                xs = [xs[n] + xs[n + 1] for n in range(0, len(xs) - 1, 2)] + xs[len(xs) & ~1:]
            return xs[0]

        def pass_sum(rows, src):
            part_scr[rows, :] = tree_sum([acc_ref[src, c] for c in cols])

        def pass_sq(rows, src):
            mu = mu_scr[rows, :]
            dev = [acc_ref[src, c] - mu for c in cols]
            part_scr[rows, :] = tree_sum([x * x for x in dev])

        def pass_norm(rows, src):
            mu = mu_scr[rows, :]
            rstd = rstd_scr[rows, :]
            for c in cols:
                out = (acc_ref[src, c] - mu) * rstd * g_ref[:, c] + beta_ref[:, c]
                o32_ref[rows, c] = out
                obf_ref[rows, c] = out.astype(BF16)

        row_groups(pass_sum)
        mu_scr[...] = lane_mean(part_scr[...])
        row_groups(pass_sq)
        rstd_scr[...] = lax.rsqrt(lane_mean(part_scr[...]) + LN_EPS)
        row_groups(pass_norm)


def _mm_res_ln(a, b, h32, g, beta, alpha):
    m, kdim = a.shape
    d = b.shape[1]
    tm, tk = _tile(m, LN_TM), _tile(kdim, LN_TK)
    nk = kdim // tk
    nh = max(n for n in range(1, min(nk, LN_H_CHUNKS) + 1) if d % (n * LANES) == 0)
    rows_out = min(tm, LN_OUT_ROWS)
    ns = tm // rows_out
    ni = m // tm
    sweep = lambda t: t >= nk
    i_of = lambda i, t: jnp.where(sweep(t), jnp.minimum(i + 1, ni - 1), i)
    k_of = lambda t: jnp.where(sweep(t), 0, t)
    out_row = lambda i, t: (i * ns + jnp.maximum(t - nk, 0), 0)
    return pl.pallas_call(
        functools.partial(_mm_res_ln_kernel, alpha=alpha, nk=nk, nh=nh),
        grid=(ni, nk + ns),
        in_specs=[pl.BlockSpec((tm, tk), lambda i, t: (i_of(i, t), k_of(t))),
                  pl.BlockSpec((tk, d), lambda i, t: (k_of(t), 0)),
                  pl.BlockSpec((tm, d // nh), lambda i, t: (i_of(i, t), jnp.minimum(k_of(t), nh - 1))),
                  pl.BlockSpec((1, d), lambda i, t: (0, 0)),
                  pl.BlockSpec((1, d), lambda i, t: (0, 0))],
        out_specs=[pl.BlockSpec((rows_out, d), out_row),
                   pl.BlockSpec((rows_out, d), out_row)],
        out_shape=[jax.ShapeDtypeStruct((m, d), F32), jax.ShapeDtypeStruct((m, d), BF16)],
        scratch_shapes=[pltpu.VMEM((tm, d), F32)] + [pltpu.VMEM((rows_out, LANES), F32)] * 3,
        compiler_params=_params("parallel", "arbitrary"),
        name="proj_res_ln",
    )(a, b, h32, g.reshape(1, d).astype(F32), beta.reshape(1, d).astype(F32))


def _ffn_up_kernel(x_ref, wg_ref, wu_ref, wd_ref, o_ref, wd_bf_ref, *, n_real, nj, wd_blocks):
    i = pl.program_id(0)
    j = pl.program_id(1)

    def cast_down_rows():
        real_rows = i * nj + j < wd_blocks
        wd_bf_ref[...] = jnp.where(real_rows, wd_ref[...], 0.0).astype(BF16)

    @pl.when(j < n_real)
    def _():
        x = x_ref[...]
        gate = _dot(x, wg_ref[...].astype(BF16))
        up = _dot(x, wu_ref[...].astype(BF16))
        o_ref[...] = (gate * _sigmoid(gate) * up).astype(o_ref.dtype)
        cast_down_rows()

    @pl.when(j >= n_real)
    def _():
        o_ref[...] = jnp.zeros_like(o_ref)
        cast_down_rows()


def _ffn_up(x, w_up_stack, w_down_stack, idx, ffn_pad):
    m, d = x.shape
    f = w_up_stack.shape[2] // 2
    d_out = w_down_stack.shape[2]
    tm, tn = _tile(m, BIG_TM), _tile(f, FFN_TN)
    assert ffn_pad % tn == 0
    n_real = f // tn
    ni, nj = m // tm, ffn_pad // tn
    wd_rows = ffn_pad // (ni * nj)
    assert wd_rows * ni * nj == ffn_pad and wd_rows % (2 * SUBLANES) == 0 and f % wd_rows == 0
    wd_blocks = f // wd_rows
    real = lambda j: jnp.minimum(j, n_real - 1)
    return pl.pallas_call(
        functools.partial(_ffn_up_kernel, n_real=n_real, nj=nj, wd_blocks=wd_blocks),
        grid=(ni, nj),
        in_specs=[pl.BlockSpec((tm, d), lambda i, j: (i, 0), pipeline_mode=pl.Buffered(1)),
                  pl.BlockSpec((None, d, tn), lambda i, j: (idx, 0, real(j))),
                  pl.BlockSpec((None, d, tn), lambda i, j: (idx, 0, n_real + real(j))),
                  pl.BlockSpec((None, wd_rows, d_out),
                               lambda i, j: (idx, jnp.minimum(i * nj + j, wd_blocks - 1), 0))],
        out_specs=[pl.BlockSpec((tm, tn), lambda i, j: (i, j)),
                   pl.BlockSpec((wd_rows, d_out), lambda i, j: (i * nj + j, 0))],
        out_shape=[jax.ShapeDtypeStruct((m, ffn_pad), BF16),
                   jax.ShapeDtypeStruct((ffn_pad, d_out), BF16)],
        compiler_params=_params("arbitrary", "arbitrary"),
        name="ffn_up",
    )(x, w_up_stack, w_up_stack, w_down_stack)


def _mlstm_kernel(q_ref, k_ref, v_ref, o_ref, gc_ref, gri_ref, grf_ref, ng_ref, out_ref,
                  ct_ref, m_ref, *, chunk, dk, dv, heads, hps):
    hp = pl.program_id(1)
    c = pl.program_id(2)
    L = chunk

    @pl.when(c == 0)
    def _():
        ct_ref[...] = jnp.zeros_like(ct_ref)
        m_ref[...] = jnp.zeros_like(m_ref)

    lane = lax.broadcasted_iota(jnp.int32, (L, LANES), 1)
    sub = lax.broadcasted_iota(jnp.int32, (SUBLANES, L), 0)
    ti = lax.broadcasted_iota(jnp.int32, (L, L), 0)
    si = lax.broadcasted_iota(jnp.int32, (L, L), 1)
    causal = si <= ti
    tri_lo = causal.astype(BF16)
    tri_up = (ti <= si).astype(BF16)
    one_lane = (lane == 0).astype(F32)

    gates_cap = _soft_cap(gc_ref[...])
    cum_cols = sum(_dot(tri_lo, p) for p in _split3(_log_sigmoid(gates_cap)))
    ig_rows = _soft_cap(gri_ref[...])
    cum_rows = sum(_dot(p, tri_up) for p in _split3(_log_sigmoid(_soft_cap(grf_ref[...]))))

    for hl in range(hps):
        h = hp * hps + hl
        kcols = slice(hl * dk, (hl + 1) * dk)
        vcols = slice(hl * dv, (hl + 1) * dv)

        ig_c = jnp.sum(jnp.where(lane == h, gates_cap, 0.0), axis=1, keepdims=True)
        b_c = jnp.sum(jnp.where(lane == heads + h, cum_cols, 0.0), axis=1, keepdims=True)
        ig_r = jnp.sum(jnp.where(sub == h, ig_rows, 0.0), axis=0, keepdims=True)
        b_r = jnp.sum(jnp.where(sub == h, cum_rows, 0.0), axis=0, keepdims=True)

        m_prev = m_ref[hl]
        d_log = jnp.where(causal, b_c - b_r + ig_r, NEG_BIG)
        g_log = b_c + m_prev
        m_t = jnp.maximum(g_log, jnp.max(d_log, axis=1, keepdims=True))
        w_intra = jnp.exp(d_log - m_t)
        w_inter = jnp.exp(g_log - m_t)

        q = q_ref[:, kcols]
        k = k_ref[:, kcols]
        v = v_ref[:, vcols]
        s = _dot_nt(q, k) * w_intra
        ct = ct_ref[hl]
        inter = _dot(q, ct.astype(BF16))
        num = _dot(s.astype(BF16), v) + w_inter * inter[:, :dv]
        den = jnp.sum(s, axis=1, keepdims=True) + w_inter * inter[:, dv:dv + 1]
        hh = num * (1.0 / jnp.maximum(jnp.abs(den), jnp.exp(-m_t)))

        b_end = b_c[L - 1:L, :]
        a_c = b_end - b_c + ig_c
        a_r = b_end - b_r + ig_r
        m_new = jnp.maximum(b_end + m_prev, jnp.max(a_r, axis=1, keepdims=True))
        decay = jnp.exp(b_end + m_prev - m_new)
        wa_c = jnp.exp(a_c - m_new)
        wv = jnp.concatenate([wa_c * v.astype(F32), wa_c * one_lane], axis=1).astype(BF16)
        ct_ref[hl] = decay * ct + _dot_tn(k, wv)
        m_ref[hl] = m_new

        hn = hh * lax.rsqrt(jnp.mean(hh * hh, axis=1, keepdims=True) + HEAD_NORM_EPS)
        out_ref[:, vcols] = (hn * ng_ref[:, vcols] * _sigmoid(o_ref[:, vcols])).astype(out_ref.dtype)


def _mlstm_core(qkv, o, gates, gates_t, norm_g, batch, seq):
    heads = MLSTM_HEADS
    assert heads == SUBLANES, "row-form gate blocks assume one sublane tile per gate kind"
    m = qkv.shape[0]
    hv = o.shape[1]
    dv = hv // heads
    dk = (qkv.shape[1] - hv) // (2 * heads)
    assert 2 * heads * dk == hv, "v blocks are addressed in units of dv"
    hps = MLSTM_HEADS_PER_STEP
    assert heads % hps == 0
    groups = heads // hps
    chunk = _tile(seq, MLSTM_CHUNK)
    nc = seq // chunk
    row = lambda b, h, c: b * nc + c
    return pl.pallas_call(
        functools.partial(_mlstm_kernel, chunk=chunk, dk=dk, dv=dv, heads=heads, hps=hps),
        grid=(batch, groups, nc),
        in_specs=[pl.BlockSpec((chunk, hps * dk), lambda b, h, c: (row(b, h, c), h)),
                  pl.BlockSpec((chunk, hps * dk), lambda b, h, c: (row(b, h, c), groups + h)),
                  pl.BlockSpec((chunk, hps * dv), lambda b, h, c: (row(b, h, c), groups + h)),
                  pl.BlockSpec((chunk, hps * dv), lambda b, h, c: (row(b, h, c), h)),
                  pl.BlockSpec((chunk, LANES), lambda b, h, c: (row(b, h, c), 0)),
                  pl.BlockSpec((SUBLANES, chunk), lambda b, h, c: (0, row(b, h, c))),
                  pl.BlockSpec((SUBLANES, chunk), lambda b, h, c: (1, row(b, h, c))),
                  pl.BlockSpec((1, hps * dv), lambda b, h, c: (0, h))],
        out_specs=pl.BlockSpec((chunk, hps * dv), lambda b, h, c: (row(b, h, c), h)),
        out_shape=jax.ShapeDtypeStruct((m, hv), BF16),
        scratch_shapes=[pltpu.VMEM((hps, dk, dv + LANES), F32), pltpu.VMEM((hps, 1, 1), F32)],
        compiler_params=_params("parallel", "parallel", "arbitrary"),
        name="mlstm_core",
    )(qkv, qkv, qkv, o, gates, gates_t, gates_t, norm_g.reshape(1, hv).astype(F32))


def _fox_cum_kernel(g_ref, cum_ref, carry_ref, *, ts):
    @pl.when(pl.program_id(1) == 0)
    def _():
        carry_ref[...] = jnp.zeros_like(carry_ref)

    logf = _log_sigmoid(g_ref[...])
    ti = lax.broadcasted_iota(jnp.int32, (ts, ts), 0)
    si = lax.broadcasted_iota(jnp.int32, (ts, ts), 1)
    tri_lo = (si <= ti).astype(BF16)
    cum = sum(_dot(tri_lo, p) for p in _split3(logf)) + carry_ref[...]
    carry_ref[...] = cum[ts - 1:ts, :]
    cum_ref[...] = cum


def _fox_cumsum(gates, batch, seq):
    ts = _tile(seq, FOX_TS)
    ns = seq // ts
    return pl.pallas_call(
        functools.partial(_fox_cum_kernel, ts=ts),
        grid=(batch, ns),
        in_specs=[pl.BlockSpec((ts, LANES), lambda b, s: (b * ns + s, 0))],
        out_specs=pl.BlockSpec((ts, LANES), lambda b, s: (b * ns + s, 0)),
        out_shape=jax.ShapeDtypeStruct((batch * seq, LANES), F32),
        scratch_shapes=[pltpu.VMEM((1, LANES), F32)],
        compiler_params=_params("parallel", "arbitrary"),
        name="fox_cumsum",
    )(gates)


def _bias_lanes(col, own_first):
    n = col.shape[0]
    hi, mid, lo = (p.astype(F32) for p in _split3(col))
    lane = lax.broadcasted_iota(jnp.int32, (n, LANES), 1)
    base = 0 if own_first else 3
    ones = ((lane >= 3 - base) & (lane < 6 - base)).astype(F32)
    out = jnp.where(lane == base, hi, jnp.where(lane == base + 1, mid, jnp.where(lane == base + 2, lo, ones)))
    return out.astype(BF16)


def _fox_attn_kernel(q_ref, k_ref, v_ref, cc_ref, o_ref, ka_scr, vt_scr, s_scr, p_scr,
                     a_scr, m_scr, l_scr, acc_scr, *, tq, seq, hps):
    hp = pl.program_id(1)
    i = pl.program_id(2)
    dh = LANES
    local_heads = range(hps)

    def head_col(rows, hl):
        lane = lax.broadcasted_iota(jnp.int32, rows.shape, 1)
        return jnp.sum(jnp.where(lane == hp * hps + hl, rows, 0.0), axis=1, keepdims=True) * LOG2E

    @pl.when(i == 0)
    def _():
        for hl in local_heads:
            hc = slice(hl * dh, (hl + 1) * dh)
            for r0 in range(0, seq, tq):
                ka_scr[hl, r0:r0 + tq, :LANES] = k_ref[r0:r0 + tq, hc]
                ka_scr[hl, r0:r0 + tq, LANES:] = _bias_lanes(-head_col(cc_ref[r0:r0 + tq, :], hl), True)
                vt_scr[hl, :, r0:r0 + tq] = v_ref[r0:r0 + tq, hc].astype(F32).T.astype(BF16)

    q0 = pl.multiple_of(i * tq, tq)
    cum_q = cc_ref[pl.ds(q0, tq), :]
    qa = [jnp.concatenate([q_ref[:, hl * dh:(hl + 1) * dh], _bias_lanes(head_col(cum_q, hl), False)], axis=1)
          for hl in local_heads]

    m_scr[...] = jnp.full_like(m_scr, NEG_BIG)
    l_scr[...] = jnp.zeros_like(l_scr)
    acc_scr[...] = jnp.zeros_like(acc_scr)

    def block_of(pos):
        pos = jnp.minimum(pos, i)
        return jnp.where(pos == 0, i, pos - 1)

    def logits(pos, parity):
        start = pl.multiple_of(block_of(pos) * tq, tq)
        for hl in local_heads:
            s_scr[hl, parity] = _dot_nt(ka_scr[hl, pl.ds(start, tq), :], qa[hl])

    def softmax(parity, on_diagonal):
        for hl in local_heads:
            st = s_scr[hl, parity]
            if on_diagonal:
                ki = lax.broadcasted_iota(jnp.int32, (tq, tq), 0)
                qi = lax.broadcasted_iota(jnp.int32, (tq, tq), 1)
                st = jnp.where(ki <= qi, st, NEG_BIG)
            m_old = m_scr[hl]
            m_new = jnp.maximum(m_old, jnp.max(st, axis=0, keepdims=True))
            p = jnp.exp2(st - m_new)
            a = jnp.exp2(m_old - m_new)
            l_scr[hl] = a * l_scr[hl] + jnp.sum(p, axis=0, keepdims=True)
            m_scr[hl] = m_new
            p_scr[hl, parity] = p.astype(BF16)
            a_scr[hl] = a

    def values(pos, parity):
        start = pl.multiple_of(block_of(pos) * tq, tq)
        for hl in local_heads:
            acc_scr[hl] = a_scr[hl] * acc_scr[hl] + _dot(vt_scr[hl, :, pl.ds(start, tq)], p_scr[hl, parity])

    def step(pos, parity):
        values(pos - 1, 1 - parity)
        logits(pos + 1, 1 - parity)
        softmax(parity, False)

    logits(0, 0)
    logits(1, 1)
    softmax(0, True)

    def pair(t, carry):
        step(2 * t + 1, 1)
        step(2 * t + 2, 0)
        return carry

    lax.fori_loop(0, i // 2, pair, 0)

    @pl.when(i % 2 == 0)
    def _():
        values(i, 0)

    @pl.when(i % 2 == 1)
    def _():
        step(i, 1)
        values(i, 1)

    for hl in local_heads:
        o_ref[:, hl * dh:(hl + 1) * dh] = (acc_scr[hl] * (1.0 / l_scr[hl])).T.astype(o_ref.dtype)


def _fox_attn(qkv, cum, batch, seq):
    dh = FOX_HEAD_DIM
    assert dh == LANES
    m = qkv.shape[0]
    heads = qkv.shape[1] // (3 * dh)
    assert heads <= LANES
    hps = FOX_HEADS_PER_STEP if heads % FOX_HEADS_PER_STEP == 0 else 1
    groups = heads // hps
    tq = _tile(seq, FOX_TQ)
    nq = seq // tq
    return pl.pallas_call(
        functools.partial(_fox_attn_kernel, tq=tq, seq=seq, hps=hps),
        grid=(batch, groups, nq),
        in_specs=[pl.BlockSpec((tq, hps * dh), lambda b, h, i: (b * nq + i, h)),
                  pl.BlockSpec((seq, hps * dh), lambda b, h, i: (b, groups + h)),
                  pl.BlockSpec((seq, hps * dh), lambda b, h, i: (b, 2 * groups + h)),
                  pl.BlockSpec((seq, LANES), lambda b, h, i: (b, 0))],
        out_specs=pl.BlockSpec((tq, hps * dh), lambda b, h, i: (b * nq + i, h)),
        out_shape=jax.ShapeDtypeStruct((m, heads * dh), BF16),
        scratch_shapes=[pltpu.VMEM((hps, seq, 2 * LANES), BF16), pltpu.VMEM((hps, dh, seq), BF16),
                        pltpu.VMEM((hps, 2, tq, tq), F32), pltpu.VMEM((hps, 2, tq, tq), BF16),
                        pltpu.VMEM((hps, 1, tq), F32), pltpu.VMEM((hps, 1, tq), F32),
                        pltpu.VMEM((hps, 1, tq), F32), pltpu.VMEM((hps, dh, tq), F32)],
        compiler_params=_params("parallel", "parallel", "arbitrary"),
        name="fox_attn",
    )(qkv, qkv, qkv, cum)


def kernel(x, mlstm_w_in, mlstm_b_gate, mlstm_norm_g, mlstm_w_out, fox_w_in, fox_b_f, fox_w_out,
           ln_mix_g, ln_mix_b, ffn_w_up, ffn_w_down, ln_ffn_g, ln_ffn_b):
    batch, seq, d = x.shape
    depth = ln_mix_g.shape[0]
    alpha = (2 * depth) ** 0.25
    m = batch * seq
    ffn = ffn_w_down.shape[1]
    ffn_pad = -(-ffn // FFN_PAD) * FFN_PAD
    hv = mlstm_w_out.shape[1]
    mlstm_main = mlstm_w_in.shape[2] - 2 * MLSTM_HEADS
    mlstm_qk = (mlstm_main - 2 * hv) // 2
    mlstm_dk = mlstm_qk // MLSTM_HEADS
    fox_w = fox_w_out.shape[1]

    mlstm_wt = jnp.swapaxes(mlstm_w_in, 1, 2)
    fox_wt = jnp.swapaxes(fox_w_in, 1, 2)

    h32 = x.reshape(m, d)
    hbf = None
    for layer in range(depth):
        slot = layer // N_MIXERS
        if layer % N_MIXERS == 0:
            gates = _gate_proj(h32, mlstm_wt, slot, mlstm_main, mlstm_b_gate[slot], emit_bf16=hbf is None)
            if hbf is None:
                gates, hbf = gates
            qkv, o, w_out = _proj(hbf, mlstm_wt, slot, mlstm_main - hv, hv, mlstm_qk, mlstm_dk ** -0.5,
                                  mlstm_w_out)
            mixed = _mlstm_core(qkv, o, gates, gates.T, mlstm_norm_g[slot], batch, seq)
        else:
            qkv, _, w_out = _proj(hbf, fox_wt, slot, 3 * fox_w, 0, fox_w, FOX_HEAD_DIM ** -0.5 * LOG2E,
                                  fox_w_out)
            gates = _gate_proj(h32, fox_wt, slot, 3 * fox_w, fox_b_f[slot])
            mixed = _fox_attn(qkv, _fox_cumsum(gates, batch, seq), batch, seq)
        h32, hbf = _mm_res_ln(mixed, w_out, h32, ln_mix_g[layer], ln_mix_b[layer], alpha)

        hidden, w_down = _ffn_up(hbf, ffn_w_up, ffn_w_down, layer, ffn_pad)
        h32, hbf = _mm_res_ln(hidden, w_down, h32, ln_ffn_g[layer], ln_ffn_b[layer], alpha)
    return h32.reshape(batch, seq, d)
```
